```python
import functools
import jax, jax.numpy as jnp
from jax import lax
import numpy as np

D_MODEL = 1024
BATCH = 8
SEQ = 4096
DEPTH = 1
DEC_BATCH = 32
DEC_SEQ = 8
PAST_LEN = 16384
PAGE_SIZE = 128

N_HEADS = 8
N_KV_HEADS = 2
HEAD_DIM = 64
GQA_GROUP = N_HEADS // N_KV_HEADS
IDX_HEADS = 8
IDX_DIM = 64
TOPK_MAX = 256
Q_BLOCK = 128
GLA_HEADS = 4
GLA_DK_TOT = D_MODEL // 2
GLA_DV_TOT = D_MODEL
GLA_DK = GLA_DK_TOT // GLA_HEADS
GLA_DV = GLA_DV_TOT // GLA_HEADS
GATE_RANK = 16
GATE_TAU = 16.0
GLA_CHUNK = 64
D_FF = 4 * D_MODEL
NORM_EPS = 1e-5
DN_ALPHA = (2 * DEPTH) ** 0.25
DN_BETA = (8 * DEPTH) ** -0.25

PROJ_SPLITS = (
    ('q_a', N_HEADS * HEAD_DIM),
    ('k_a', N_KV_HEADS * HEAD_DIM),
    ('v_a', N_KV_HEADS * HEAD_DIM),
    ('q_i', IDX_HEADS * IDX_DIM),
    ('k_i', IDX_DIM),
    ('w_i', IDX_HEADS),
    ('q_b', GLA_DK_TOT),
    ('k_b', GLA_DK_TOT),
    ('v_b', GLA_DV_TOT),
    ('g_b', GLA_DV_TOT),
    ('a_b', GATE_RANK),
    ('gate_a', D_MODEL),
    ('gate_b', D_MODEL),
)
D_IN = (N_HEADS * HEAD_DIM + 2 * N_KV_HEADS * HEAD_DIM + IDX_HEADS * IDX_DIM + IDX_DIM + IDX_HEADS
        + 2 * GLA_DK_TOT + 2 * GLA_DV_TOT + GATE_RANK + 2 * D_MODEL)

kernel_name = 'dsa_gla_gated_hybrid_step'


def layer_norm(x, g, b):
    xf = x.astype(jnp.float32)
    xc = xf - jnp.mean(xf, -1, keepdims=True)
    var = jnp.mean(xc * xc, -1, keepdims=True)
    return (xc * lax.rsqrt(var + NORM_EPS) * g + b).astype(x.dtype)


def split_projection(z):
    parts = {}
    off = 0
    for name, width in PROJ_SPLITS:
        parts[name] = z[..., off:off + width]
        off += width
    return parts


def gather_rows(rows, idx):
    return jax.vmap(lambda r, i: r[i])(rows, idx)


def indexer_scores(q_idx, w_idx, k_idx):
    dots = jnp.einsum('bqhd,bsd->bqhs', q_idx, k_idx).astype(jnp.float32) * (IDX_DIM ** -0.5)
    w = w_idx.astype(jnp.float32) * (IDX_HEADS ** -0.5)
    return jnp.einsum('bqhs,bqh->bqs', jax.nn.relu(dots), w)


def select_keys(scores, q_pos, topk):
    k_pos = jnp.arange(scores.shape[-1])
    admissible = k_pos[None, None, :] <= q_pos[None, :, None]
    scores = jnp.where(admissible, scores, -jnp.inf)
    _, idx = lax.top_k(scores, topk)
    valid = idx <= q_pos[None, :, None]
    return idx, valid


def sparse_attend(q, k_sel, v_sel, valid):
    b, t = q.shape[:2]
    qg = q.reshape(b, t, N_KV_HEADS, GQA_GROUP, HEAD_DIM)
    s = jnp.einsum('bqkgd,bqskd->bqkgs', qg, k_sel).astype(jnp.float32) * (HEAD_DIM ** -0.5)
    s = jnp.where(valid[:, :, None, None, :], s, -jnp.inf)
    p = jax.nn.softmax(s, axis=-1).astype(v_sel.dtype)
    o = jnp.einsum('bqkgs,bqskd->bqkgd', p, v_sel)
    return o.reshape(b, t, N_HEADS * HEAD_DIM)


def dsa_prompt(q, k, v, q_idx, k_idx, w_idx):
    b, t = q.shape[:2]
    topk = min(TOPK_MAX, t // 4)
    qb = min(Q_BLOCK, t)

    def block(i):
        start = i * qb
        q_blk = lax.dynamic_slice_in_dim(q, start, qb, axis=1)
        qi_blk = lax.dynamic_slice_in_dim(q_idx, start, qb, axis=1)
        wi_blk = lax.dynamic_slice_in_dim(w_idx, start, qb, axis=1)
        q_pos = start + jnp.arange(qb)
        idx, valid = select_keys(indexer_scores(qi_blk, wi_blk, k_idx), q_pos, topk)
        return sparse_attend(q_blk, gather_rows(k, idx), gather_rows(v, idx), valid)

    o = lax.map(block, jnp.arange(t // qb))
    return o.transpose(1, 0, 2, 3).reshape(b, t, N_HEADS * HEAD_DIM)


def dsa_sample(q, k, v, q_idx, k_idx, w_idx, cache_k, cache_v, cache_kidx, page_table):
    b, t = q.shape[:2]
    past = page_table.shape[1] * PAGE_SIZE
    topk = min(TOPK_MAX, (past + t) // 4)
    k_idx_past = cache_kidx[page_table].reshape(b, past, IDX_DIM).astype(k_idx.dtype)
    k_idx_all = jnp.concatenate([k_idx_past, k_idx], axis=1)
    q_pos = past + jnp.arange(t)
    idx, valid = select_keys(indexer_scores(q_idx, w_idx, k_idx_all), q_pos, topk)
    in_past = (idx < past)[..., None, None]
    p_idx = jnp.minimum(idx, past - 1)
    phys = page_table[jnp.arange(b)[:, None, None], p_idx // PAGE_SIZE]
    off = p_idx % PAGE_SIZE
    n_idx = jnp.clip(idx - past, 0, t - 1)
    k_sel = jnp.where(in_past, cache_k[phys, off].astype(k.dtype), gather_rows(k, n_idx))
    v_sel = jnp.where(in_past, cache_v[phys, off].astype(v.dtype), gather_rows(v, n_idx))
    return sparse_attend(q, k_sel, v_sel, valid)


def gla_chunked(q, k, v, log_a, s0, chunk):
    b, t, h, _ = q.shape
    n = t // chunk

    def to_chunks(u):
        return u.astype(jnp.float32).reshape(b, n, chunk, h, -1).transpose(1, 0, 3, 2, 4)

    causal = jnp.tril(jnp.ones((chunk, chunk), dtype=bool))

    def step(s, inp):
        qx, kx, vx, ax = inp
        cum = jnp.cumsum(ax, axis=-2)
        last = cum[..., -1:, :]
        q_t = qx * jnp.exp(cum)
        k_t = kx * jnp.exp(-cum)
        att = jnp.where(causal, jnp.einsum('bhcd,bhed->bhce', q_t, k_t), 0.0)
        o = jnp.einsum('bhce,bhev->bhcv', att, vx) + jnp.einsum('bhcd,bhdv->bhcv', q_t, s)
        s = jnp.exp(last)[..., 0, :, None] * s + jnp.einsum('bhcd,bhcv->bhdv', kx * jnp.exp(last - cum), vx)
        return s, o

    s_fin, o = lax.scan(step, s0.astype(jnp.float32), (to_chunks(q), to_chunks(k), to_chunks(v), to_chunks(log_a)))
    o = o.transpose(1, 0, 3, 2, 4).reshape(b, t, h, v.shape[-1])
    return o, s_fin


def token_mixer(x, w_in, w_alpha2, b_alpha, gla_norm_g, w_attn_o, w_gla_o, w_out, attend, gla_s0, gla_chunk):
    b, t = x.shape[:2]
    z = split_projection(x @ w_in)
    q_a = z['q_a'].reshape(b, t, N_HEADS, HEAD_DIM)
    k_a = z['k_a'].reshape(b, t, N_KV_HEADS, HEAD_DIM)
    v_a = z['v_a'].reshape(b, t, N_KV_HEADS, HEAD_DIM)
    q_i = z['q_i'].reshape(b, t, IDX_HEADS, IDX_DIM)
    k_i = z['k_i']
    attn = attend(q_a, k_a, v_a, q_i, k_i, z['w_i'])
    log_a = jax.nn.log_sigmoid((z['a_b'] @ w_alpha2 + b_alpha).astype(jnp.float32)) / GATE_TAU
    q_b = z['q_b'].reshape(b, t, GLA_HEADS, GLA_DK) * (GLA_DK ** -0.5)
    k_b = z['k_b'].reshape(b, t, GLA_HEADS, GLA_DK)
    v_b = z['v_b'].reshape(b, t, GLA_HEADS, GLA_DV)
    o_b, s_new = gla_chunked(q_b, k_b, v_b, log_a.reshape(b, t, GLA_HEADS, GLA_DK), gla_s0, gla_chunk)
    o_b = o_b * lax.rsqrt(jnp.mean(o_b * o_b, -1, keepdims=True) + NORM_EPS) * gla_norm_g
    o_b = (o_b.reshape(b, t, GLA_DV_TOT) * jax.nn.silu(z['g_b'].astype(jnp.float32))).astype(x.dtype)
    branch_a = attn @ w_attn_o
    branch_b = o_b @ w_gla_o
    merged = jax.nn.sigmoid(z['gate_a']) * branch_a + jax.nn.sigmoid(z['gate_b']) * branch_b
    return merged @ w_out, k_a, v_a, k_i, s_new.astype(x.dtype)


def residual_block(x, mix_out, ln1_g, ln1_b, w_ff1, w_ff2, ln2_g, ln2_b):
    h = layer_norm(DN_ALPHA * x + mix_out, ln1_g, ln1_b)
    ff = jnp.square(jax.nn.relu(h @ w_ff1)) @ w_ff2
    return layer_norm(DN_ALPHA * h + ff, ln2_g, ln2_b)


def setup_inputs(seed: int = 0) -> dict:
    key = jax.random.key(seed)
    ks = jax.random.split(key, 24)
    f32 = jnp.float32
    n_pages = PAST_LEN // PAGE_SIZE
    used = DEC_BATCH * n_pages
    n_pool = used + max(1, used // 4)

    def nrm(k, shape, scale):
        return jax.random.normal(k, shape, f32) * scale

    page_table = jax.random.permutation(ks[0], n_pool)[:used].reshape(DEC_BATCH, n_pages).astype(jnp.int32)
    return {
        'x_prompt': nrm(ks[1], (BATCH, SEQ, D_MODEL), 1.0),
        'x_sample': nrm(ks[2], (DEC_BATCH, DEC_SEQ, D_MODEL), 1.0),
        'cache_k': nrm(ks[3], (DEPTH, n_pool, PAGE_SIZE, N_KV_HEADS, HEAD_DIM), 1.0),
        'cache_v': nrm(ks[4], (DEPTH, n_pool, PAGE_SIZE, N_KV_HEADS, HEAD_DIM), 1.0),
        'cache_kidx': nrm(ks[5], (DEPTH, n_pool, PAGE_SIZE, IDX_DIM), 1.0),
        'state_gla': nrm(ks[6], (DEPTH, DEC_BATCH, GLA_HEADS, GLA_DK, GLA_DV), 0.5),
        'page_table': page_table,
        'w_in': nrm(ks[7], (DEPTH, D_MODEL, D_IN), D_MODEL ** -0.5),
        'w_alpha2': nrm(ks[8], (DEPTH, GATE_RANK, GLA_DK_TOT), GATE_RANK ** -0.5),
        'b_alpha': nrm(ks[9], (DEPTH, GLA_DK_TOT), 0.1),
        'gla_norm_g': 1.0 + nrm(ks[10], (DEPTH, GLA_DV), 0.02),
        'w_attn_o': nrm(ks[11], (DEPTH, N_HEADS * HEAD_DIM, D_MODEL), (N_HEADS * HEAD_DIM) ** -0.5),
        'w_gla_o': nrm(ks[12], (DEPTH, GLA_DV_TOT, D_MODEL), GLA_DV_TOT ** -0.5),
        'w_out': nrm(ks[13], (DEPTH, D_MODEL, D_MODEL), DN_BETA * D_MODEL ** -0.5),
        'ln1_g': 1.0 + nrm(ks[14], (DEPTH, D_MODEL), 0.02),
        'ln1_b': nrm(ks[15], (DEPTH, D_MODEL), 0.02),
        'w_ff1': nrm(ks[16], (DEPTH, D_MODEL, D_FF), D_MODEL ** -0.5),
        'w_ff2': nrm(ks[17], (DEPTH, D_FF, D_MODEL), DN_BETA * D_FF ** -0.5),
        'ln2_g': 1.0 + nrm(ks[18], (DEPTH, D_MODEL), 0.02),
        'ln2_b': nrm(ks[19], (DEPTH, D_MODEL), 0.02),
    }


def reference(x_prompt, x_sample, cache_k, cache_v, cache_kidx, state_gla, page_table,
              w_in, w_alpha2, b_alpha, gla_norm_g, w_attn_o, w_gla_o, w_out,
              ln1_g, ln1_b, w_ff1, w_ff2, ln2_g, ln2_b):
    h_p, h_s = x_prompt, x_sample
    kp_l, vp_l, kip_l, sp_l = [], [], [], []
    ks_l, vs_l, kis_l, ss_l = [], [], [], []
    for l in range(DEPTH):
        mix_w = (w_in[l], w_alpha2[l], b_alpha[l], gla_norm_g[l], w_attn_o[l], w_gla_o[l], w_out[l])
        s0 = jnp.zeros((h_p.shape[0], GLA_HEADS, GLA_DK, GLA_DV), jnp.float32)
        m_p, k_p, v_p, ki_p, s_p = token_mixer(h_p, *mix_w, dsa_prompt, s0, min(GLA_CHUNK, h_p.shape[1]))
        attend_s = functools.partial(dsa_sample, cache_k=cache_k[l], cache_v=cache_v[l],
                                     cache_kidx=cache_kidx[l], page_table=page_table)
        m_s, k_s, v_s, ki_s, s_s = token_mixer(h_s, *mix_w, attend_s, state_gla[l], h_s.shape[1])
        h_p = residual_block(h_p, m_p, ln1_g[l], ln1_b[l], w_ff1[l], w_ff2[l], ln2_g[l], ln2_b[l])
        h_s = residual_block(h_s, m_s, ln1_g[l], ln1_b[l], w_ff1[l], w_ff2[l], ln2_g[l], ln2_b[l])
        kp_l.append(k_p); vp_l.append(v_p); kip_l.append(ki_p); sp_l.append(s_p)
        ks_l.append(k_s); vs_l.append(v_s); kis_l.append(ki_s); ss_l.append(s_s)
    return (h_p, h_s,
            jnp.stack(kp_l), jnp.stack(vp_l), jnp.stack(kip_l), jnp.stack(sp_l),
            jnp.stack(ks_l), jnp.stack(vs_l), jnp.stack(kis_l), jnp.stack(ss_l))
```

```python
import functools

import jax
import jax.numpy as jnp
from jax import lax
from jax.experimental import pallas as pl
from jax.experimental.pallas import tpu as pltpu

D_MODEL = 1024
N_HEADS = 8
N_KV_HEADS = 2
HEAD_DIM = 64
GQA_GROUP = N_HEADS // N_KV_HEADS
IDX_HEADS = 8
IDX_DIM = 64
TOPK_MAX = 256
PAGE_SIZE = 128
GLA_HEADS = 4
GLA_DK_TOT = D_MODEL // 2
GLA_DV_TOT = D_MODEL
GLA_DK = GLA_DK_TOT // GLA_HEADS
GLA_DV = GLA_DV_TOT // GLA_HEADS
GATE_RANK = 16
GATE_TAU = 16.0
GLA_CHUNK = 64
D_FF = 4 * D_MODEL
NORM_EPS = 1e-5
DEPTH = 1
DN_ALPHA = (2 * DEPTH) ** 0.25

LANE = 128
SUBLANE = 8
VMEM_LIMIT = 56 * 1024 * 1024
BF = jnp.bfloat16
F32 = jnp.float32
I32 = jnp.int32
INT_MIN = -2 ** 31
NEG_BIG = -1e30

MISC_WI = IDX_DIM
MISC_AB = IDX_DIM + IDX_HEADS

_PROJ_GROUPS = (
    ('qa', N_HEADS * LANE), ('ka', LANE), ('va', LANE), ('qi', IDX_HEADS * LANE), ('misc', LANE),
    ('qb', GLA_DK_TOT), ('kb', GLA_DK_TOT), ('vb', GLA_DV_TOT), ('gb', GLA_DV_TOT),
    ('ga', D_MODEL), ('gb2', D_MODEL),
)
_PROJ_OFF = {}
_o = 0
for _n, _w in _PROJ_GROUPS:
    _PROJ_OFF[_n] = (_o, _o + _w)
    _o += _w
PROJ_COLS = _o


def _const_spec(shape):
    nd = len(shape)
    return pl.BlockSpec(shape, lambda *_: (0,) * nd, pipeline_mode=pl.Buffered(1))


def _pack_w_in(w_in):
    off = 0
    parts = {}
    for name, width in (('q_a', 512), ('k_a', 128), ('v_a', 128), ('q_i', 512), ('k_i', 64), ('w_i', 8),
                        ('q_b', 512), ('k_b', 512), ('v_b', 1024), ('g_b', 1024), ('a_b', 16),
                        ('gate_a', 1024), ('gate_b', 1024)):
        parts[name] = w_in[:, off:off + width]
        off += width
    z64 = jnp.zeros((D_MODEL, HEAD_DIM), F32)
    qa_cols = []
    for h in range(N_HEADS):
        q = parts['q_a'][:, h * HEAD_DIM:(h + 1) * HEAD_DIM] * (HEAD_DIM ** -0.5)
        qa_cols += [q, z64] if h // GQA_GROUP == 0 else [z64, q]
    qi_cols = []
    for h in range(IDX_HEADS):
        qi_cols += [parts['q_i'][:, h * IDX_DIM:(h + 1) * IDX_DIM] * (IDX_DIM ** -0.5), z64]
    misc = jnp.concatenate([parts['k_i'], parts['w_i'], parts['a_b'],
                            jnp.zeros((D_MODEL, LANE - MISC_AB - GATE_RANK), F32)], axis=1)
    cols = qa_cols + [parts['k_a'], parts['v_a']] + qi_cols + [
        misc, parts['q_b'], parts['k_b'], parts['v_b'], parts['g_b'], parts['gate_a'], parts['gate_b']]
    return jnp.concatenate(cols, axis=1).astype(BF)


def _proj_kernel(x_ref, w_ref, qa_ref, ka_ref, va_ref, kab_ref, vab_ref, qi_ref, misc_ref, miscb_ref,
                 qb_ref, kb_ref, vb_ref, gb_ref, ga_ref, gb2_ref):
    xb = x_ref[...].astype(BF)

    def mm(name):
        a, b = _PROJ_OFF[name]
        return jnp.dot(xb, w_ref[:, a:b], preferred_element_type=F32)

    qa_ref[...] = mm('qa').astype(BF)
    ka = mm('ka')
    ka_ref[...] = ka
    kab_ref[...] = ka.astype(BF)
    va = mm('va')
    va_ref[...] = va
    vab_ref[...] = va.astype(BF)
    qi_ref[...] = mm('qi').astype(BF)
    misc = mm('misc')
    misc_ref[...] = misc
    miscb_ref[...] = misc.astype(BF)
    qb_ref[...] = mm('qb')
    kb_ref[...] = mm('kb')
    vb_ref[...] = mm('vb')
    gb_ref[...] = mm('gb')
    ga_ref[...] = mm('ga')
    gb2_ref[...] = mm('gb2')


def _proj(x2d, w_packed, tm):
    m = x2d.shape[0]
    outs = (('qa', BF), ('ka', F32), ('va', F32), ('ka', BF), ('va', BF), ('qi', BF), ('misc', F32), ('misc', BF),
            ('qb', F32), ('kb', F32), ('vb', F32), ('gb', F32), ('ga', F32), ('gb2', F32))
    widths = dict(_PROJ_GROUPS)
    return pl.pallas_call(
        _proj_kernel,
        grid=(m // tm,),
        in_specs=[pl.BlockSpec((tm, D_MODEL), lambda i: (i, 0)), _const_spec((D_MODEL, PROJ_COLS))],
        out_specs=[pl.BlockSpec((tm, widths[n]), lambda i: (i, 0)) for n, _ in outs],
        out_shape=[jax.ShapeDtypeStruct((m, widths[n]), dt) for n, dt in outs],
        compiler_params=pltpu.CompilerParams(dimension_semantics=("parallel",), vmem_limit_bytes=VMEM_LIMIT),
        name="proj",
    )(x2d, w_packed)


def _score_keys(scores):
    bits = pltpu.bitcast(scores, I32)
    return jnp.where(bits < 0, INT_MIN - bits, bits)


def _lane_fold(x):
    acc = x[:, :LANE]
    for j in range(1, x.shape[1] // LANE):
        acc = acc + x[:, j * LANE:(j + 1) * LANE]
    return acc


def _kth_largest_key(count_ge, rows, topk):
    def body(it, ans):
        cand = ans + jnp.left_shift(jnp.int32(1), 31 - it)
        return jnp.where(count_ge(cand) >= topk, cand, ans)
    return lax.fori_loop(0, 32, body, jnp.full((rows, 1), INT_MIN, I32))


def _tie_cut(count_tie_below, need, rows, nbits):
    def body(it, m):
        cand = m | jnp.left_shift(jnp.int32(1), nbits - 1 - it)
        return jnp.where(count_tie_below(cand) < need, cand, m)
    return lax.fori_loop(0, nbits, body, jnp.zeros((rows, 1), I32))


def _dsa_prompt_kernel(qi_ref, misc_ref, kib_ref, qa_ref, kab_ref, vab_ref, out_ref,
                       keys_ref, m_ref, l_ref, acc_ref, *, tq, kc, topk, seq):
    i = pl.program_id(1)
    nkc = ((i + 1) * tq + kc - 1) // kc
    tpos = i * tq + lax.broadcasted_iota(I32, (tq, kc), 0)
    lane_pos = lax.broadcasted_iota(I32, (tq, kc), 1)
    w = misc_ref[:, MISC_WI:MISC_WI + IDX_HEADS] * (IDX_HEADS ** -0.5)

    def score_chunk(c, carry):
        kchunk = kib_ref[pl.ds(pl.multiple_of(c * kc, kc), kc), :]
        acc = jnp.zeros((tq, kc), F32)
        for h in range(IDX_HEADS):
            d = lax.dot_general(qi_ref[:, h * LANE:(h + 1) * LANE], kchunk, (((1,), (1,)), ((), ())),
                                preferred_element_type=F32)
            acc = acc + jnp.maximum(d, 0.0) * w[:, h:h + 1]
        keys_ref[c] = jnp.where(c * kc + lane_pos <= tpos, _score_keys(acc), INT_MIN)
        return carry
    lax.fori_loop(0, nkc, score_chunk, 0)

    def count_where(indicator):
        def body(c, cnt):
            return cnt + _lane_fold(indicator(c, keys_ref[c]))
        cnt = lax.fori_loop(0, nkc, body, jnp.zeros((tq, LANE), I32))
        return jnp.sum(cnt, axis=1, keepdims=True)

    thr = _kth_largest_key(lambda cand: count_where(lambda c, k: jnp.where(k >= cand, 1, 0)), tq, topk)
    n_ge = count_where(lambda c, k: jnp.where(k >= thr, 1, 0))
    surplus = jnp.where(n_ge > topk, jnp.where(thr != INT_MIN, 1, 0), 0)

    @pl.when(jnp.max(surplus) > 0)
    def _():
        need = topk - count_where(lambda c, k: jnp.where(k > thr, 1, 0))
        nbits = max(1, (seq - 1).bit_length())
        cut = _tie_cut(lambda cand: count_where(
            lambda c, k: jnp.where(k == thr, jnp.where(c * kc + lane_pos < cand, 1, 0), 0)), need, tq, nbits)

        def drop(c, carry):
            k = keys_ref[c]
            keys_ref[c] = k - jnp.where(k == thr, jnp.where(c * kc + lane_pos > cut, surplus, 0), 0)
            return carry
        lax.fori_loop(0, nkc, drop, 0)

    thr = jnp.maximum(thr, INT_MIN + 1)

    m_ref[...] = jnp.full(m_ref.shape, NEG_BIG, F32)
    l_ref[...] = jnp.zeros(l_ref.shape, F32)
    acc_ref[...] = jnp.zeros(acc_ref.shape, F32)
    qg = [jnp.concatenate([qa_ref[:, (g * GQA_GROUP + j) * LANE:(g * GQA_GROUP + j + 1) * LANE]
                           for j in range(GQA_GROUP)], axis=0) for g in range(N_KV_HEADS)]

    def attend_chunk(c, carry):
        rows = pl.ds(pl.multiple_of(c * kc, kc), kc)
        kchunk = kab_ref[rows, :]
        vchunk = vab_ref[rows, :]
        sel = keys_ref[c] >= thr
        for g in range(N_KV_HEADS):
            s = lax.dot_general(qg[g], kchunk, (((1,), (1,)), ((), ())), preferred_element_type=F32)
            s = jnp.where(sel[None], s.reshape(GQA_GROUP, tq, kc), -jnp.inf).reshape(GQA_GROUP * tq, kc)
            m_old = m_ref[g]
            m_new = jnp.maximum(m_old, jnp.max(s, axis=1, keepdims=True))
            alpha = jnp.exp(m_old - m_new)
            p = jnp.exp(s - m_new)
            l_ref[g] = alpha * l_ref[g] + jnp.sum(p, axis=1, keepdims=True)
            acc_ref[g] = alpha * acc_ref[g] + jnp.dot(p.astype(BF), vchunk, preferred_element_type=F32)
            m_ref[g] = m_new
        return carry
    lax.fori_loop(0, nkc, attend_chunk, 0)

    for g in range(N_KV_HEADS):
        o = acc_ref[g] / l_ref[g]
        for j in range(GQA_GROUP):
            h = g * GQA_GROUP + j
            out_ref[:, h * HEAD_DIM:(h + 1) * HEAD_DIM] = (
                o[j * tq:(j + 1) * tq, g * HEAD_DIM:(g + 1) * HEAD_DIM].astype(out_ref.dtype))


def _dsa_prompt(qi, misc, miscb, qa, kab, vab, batch, seq, tq, kc):
    topk = min(TOPK_MAX, seq // 4)
    nq = seq // tq
    kern = functools.partial(_dsa_prompt_kernel, tq=tq, kc=kc, topk=topk, seq=seq)
    qrow = lambda b, i: (b * nq + i, 0)
    brow = lambda b, i: (b, 0)
    return pl.pallas_call(
        kern,
        grid=(batch, nq),
        in_specs=[pl.BlockSpec((tq, IDX_HEADS * LANE), qrow), pl.BlockSpec((tq, LANE), qrow),
                  pl.BlockSpec((seq, LANE), brow), pl.BlockSpec((tq, N_HEADS * LANE), qrow),
                  pl.BlockSpec((seq, LANE), brow), pl.BlockSpec((seq, LANE), brow)],
        out_specs=pl.BlockSpec((tq, N_HEADS * HEAD_DIM), qrow),
        out_shape=jax.ShapeDtypeStruct((batch * seq, N_HEADS * HEAD_DIM), BF),
        scratch_shapes=[pltpu.VMEM((seq // kc, tq, kc), I32),
                        pltpu.VMEM((N_KV_HEADS, GQA_GROUP * tq, 1), F32),
                        pltpu.VMEM((N_KV_HEADS, GQA_GROUP * tq, 1), F32),
                        pltpu.VMEM((N_KV_HEADS, GQA_GROUP * tq, LANE), F32)],
        compiler_params=pltpu.CompilerParams(dimension_semantics=("parallel", "arbitrary"),
                                             vmem_limit_bytes=VMEM_LIMIT),
        name="dsa_prompt",
    )(qi, misc, miscb, qa, kab, vab)


def _log_sigmoid(x):
    return jnp.minimum(x, 0.0) - jnp.log1p(jnp.exp(-jnp.abs(x)))


def _pad_rows(x, n):
    if x.shape[0] == n:
        return x
    return jnp.concatenate([x, jnp.zeros((n - x.shape[0], x.shape[1]), x.dtype)], axis=0)


def _gla_kernel(*refs, chunk, nchunks, has_s0):
    if has_s0:
        (misc_ref, wal_ref, bal_ref, q_ref, k_ref, v_ref, g_ref, gn_ref, s0_ref, o_ref, sout_ref, s_ref) = refs
    else:
        (misc_ref, wal_ref, bal_ref, q_ref, k_ref, v_ref, g_ref, gn_ref, o_ref, sout_ref, s_ref) = refs
        s0_ref = None
    t = pl.program_id(2)
    rows = chunk * nchunks
    cpad = max(chunk, 2 * SUBLANE)
    rpad = max(rows, 2 * SUBLANE)

    @pl.when(t == 0)
    def _():
        s_ref[...] = jnp.zeros(s_ref.shape, F32) if s0_ref is None else s0_ref[0, 0]

    la = jnp.dot(_pad_rows(misc_ref[0], rpad).astype(BF), wal_ref[0], preferred_element_type=F32)[:rows]
    la = _log_sigmoid(la + bal_ref[0]) / GATE_TAU
    pos = lax.broadcasted_iota(I32, (rows, GLA_DK), 0) % chunk
    cum = la
    d = 1
    while d < chunk:
        cum = cum + jnp.where(pos >= d, pltpu.roll(cum, d, axis=0), 0.0)
        d *= 2

    causal = (lax.broadcasted_iota(I32, (cpad, cpad), 0) >= lax.broadcasted_iota(I32, (cpad, cpad), 1))
    gn = gn_ref[...]
    for c in range(nchunks):
        r = slice(c * chunk, (c + 1) * chunk)
        cum_c = cum[r]
        last = cum_c[chunk - 1:chunk]
        q_c = q_ref[0, r, :] * (GLA_DK ** -0.5)
        k_c = k_ref[0, r, :]
        v_b = _pad_rows(v_ref[0, r, :], cpad).astype(BF)
        qt = _pad_rows(q_c * jnp.exp(cum_c), cpad).astype(BF)
        kt = _pad_rows(k_c * jnp.exp(-cum_c), cpad).astype(BF)
        kd = _pad_rows(k_c * jnp.exp(last - cum_c), cpad).astype(BF)
        att = lax.dot_general(qt, kt, (((1,), (1,)), ((), ())), preferred_element_type=F32)
        att = jnp.where(causal, att, 0.0).astype(BF)
        s_old = s_ref[...]
        o = (jnp.dot(att, v_b, preferred_element_type=F32)
             + jnp.dot(qt, s_old.astype(BF), preferred_element_type=F32))[:chunk]
        decay = jnp.transpose(jnp.broadcast_to(jnp.exp(last), (GLA_DK, GLA_DK)))
        upd = lax.dot_general(kd, v_b, (((0,), (0,)), ((), ())), preferred_element_type=F32)
        s_ref[...] = jnp.concatenate([decay] * (GLA_DV // GLA_DK), axis=1) * s_old + upd
        o = o * lax.rsqrt(jnp.mean(o * o, axis=-1, keepdims=True) + NORM_EPS) * gn
        gate = g_ref[0, r, :]
        o_ref[0, r, :] = (o * (gate * jax.nn.sigmoid(gate))).astype(o_ref.dtype)

    @pl.when(t == pl.num_programs(2) - 1)
    def _():
        sout_ref[0, 0] = s_ref[...]


def _gla(misc, wal, bal, qb, kb, vb, gb, gn, s0, chunk, nchunks, out_dtype):
    batch, seq, _ = misc.shape
    rows = chunk * nchunks
    head = lambda b, h, t: (b, t, h)
    in_specs = [pl.BlockSpec((1, rows, LANE), lambda b, h, t: (b, t, 0)),
                pl.BlockSpec((1, LANE, GLA_DK), lambda b, h, t: (h, 0, 0)),
                pl.BlockSpec((1, 1, GLA_DK), lambda b, h, t: (h, 0, 0)),
                pl.BlockSpec((1, rows, GLA_DK), head), pl.BlockSpec((1, rows, GLA_DK), head),
                pl.BlockSpec((1, rows, GLA_DV), head), pl.BlockSpec((1, rows, GLA_DV), head),
                pl.BlockSpec((1, GLA_DV), lambda b, h, t: (0, 0))]
    args = [misc, wal, bal, qb, kb, vb, gb, gn]
    state_spec = pl.BlockSpec((1, 1, GLA_DK, GLA_DV), lambda b, h, t: (b, h, 0, 0))
    if s0 is not None:
        in_specs.append(state_spec)
        args.append(s0)
    kern = functools.partial(_gla_kernel, chunk=chunk, nchunks=nchunks, has_s0=s0 is not None)
    return pl.pallas_call(
        kern,
        grid=(batch, GLA_HEADS, seq // rows),
        in_specs=in_specs,
        out_specs=[pl.BlockSpec((1, rows, GLA_DV), head), state_spec],
        out_shape=[jax.ShapeDtypeStruct((batch, seq, GLA_DV_TOT), out_dtype),
                   jax.ShapeDtypeStruct((batch, GLA_HEADS, GLA_DK, GLA_DV), F32)],
        scratch_shapes=[pltpu.VMEM((GLA_DK, GLA_DV), F32)],
        compiler_params=pltpu.CompilerParams(dimension_semantics=("parallel", "parallel", "arbitrary"),
                                             vmem_limit_bytes=VMEM_LIMIT),
        name="gla",
    )(*args)


def _layer_norm(x, g, b):
    xc = x - jnp.mean(x, axis=-1, keepdims=True)
    var = jnp.mean(xc * xc, axis=-1, keepdims=True)
    return xc * lax.rsqrt(var + NORM_EPS) * g + b


def _post_kernel(x_ref, attn_ref, ob_ref, ga_ref, gb_ref, wao_ref, wgo_ref, wout_ref, ln1g_ref, ln1b_ref,
                 wf1_ref, wf2_ref, ln2g_ref, ln2b_ref, out_ref, *, ff_chunk):
    branch_a = jnp.dot(attn_ref[...], wao_ref[...], preferred_element_type=F32)
    branch_b = jnp.dot(ob_ref[...], wgo_ref[...], preferred_element_type=F32)
    merged = jax.nn.sigmoid(ga_ref[...]) * branch_a + jax.nn.sigmoid(gb_ref[...]) * branch_b
    mix = jnp.dot(merged.astype(BF), wout_ref[...], preferred_element_type=F32)
    h = _layer_norm(DN_ALPHA * x_ref[...] + mix, ln1g_ref[...], ln1b_ref[...])
    hb = h.astype(BF)
    ff = jnp.zeros(h.shape, F32)
    for c in range(D_FF // ff_chunk):
        cols = slice(c * ff_chunk, (c + 1) * ff_chunk)
        a = jnp.maximum(jnp.dot(hb, wf1_ref[:, cols], preferred_element_type=F32), 0.0)
        ff = ff + jnp.dot((a * a).astype(BF), wf2_ref[cols, :], preferred_element_type=F32)
    out_ref[...] = _layer_norm(DN_ALPHA * h + ff, ln2g_ref[...], ln2b_ref[...])


def _post(x2d, attn, ob, ga, gb, wao, wgo, wout, ln1g, ln1b, wf1, wf2, ln2g, ln2b, tm):
    m = x2d.shape[0]
    row = lambda width: pl.BlockSpec((tm, width), lambda i: (i, 0))
    return pl.pallas_call(
        functools.partial(_post_kernel, ff_chunk=1024),
        grid=(m // tm,),
        in_specs=[row(D_MODEL), row(N_HEADS * HEAD_DIM), row(GLA_DV_TOT), row(D_MODEL), row(D_MODEL),
                  _const_spec(wao.shape), _const_spec(wgo.shape), _const_spec(wout.shape),
                  _const_spec(ln1g.shape), _const_spec(ln1b.shape), _const_spec(wf1.shape), _const_spec(wf2.shape),
                  _const_spec(ln2g.shape), _const_spec(ln2b.shape)],
        out_specs=row(D_MODEL),
        out_shape=jax.ShapeDtypeStruct((m, D_MODEL), F32),
        compiler_params=pltpu.CompilerParams(dimension_semantics=("parallel",), vmem_limit_bytes=VMEM_LIMIT),
        name="post",
    )(x2d, attn, ob, ga, gb, wao, wgo, wout, ln1g, ln1b, wf1, wf2, ln2g, ln2b)


def _dsa_sample_kernel(pt_ref, qi_ref, misc_ref, qa_ref, ka_ref, va_ref,
                       ckidx_ref, ck_ref, cv_ref, out_ref,
                       ibuf, kbuf, vbuf, isem, ksem, vsem, keys_ref, *, t, n_pages, ppc, topk):
    b = pl.program_id(0)
    nch = n_pages // ppc
    kc = ppc * PAGE_SIZE
    past = n_pages * PAGE_SIZE
    hq = IDX_HEADS * t

    def page_copy(src_ref, buf, sem, ch, slot, p):
        return pltpu.make_async_copy(src_ref.at[pt_ref[b, ch * ppc + p]], buf.at[slot, p], sem.at[slot])

    def start_chunk(src_ref, buf, sem, ch, slot):
        for p in range(ppc):
            page_copy(src_ref, buf, sem, ch, slot, p).start()

    def wait_chunk(src_ref, buf, sem, ch, slot):
        for p in range(ppc):
            page_copy(src_ref, buf, sem, ch, slot, p).wait()

    qi_all = jnp.concatenate([qi_ref[0, :, h * LANE:h * LANE + IDX_DIM] for h in range(IDX_HEADS)],
                             axis=0).astype(BF)
    misc = misc_ref[0]
    w = misc[:, MISC_WI:MISC_WI + IDX_HEADS] * (IDX_HEADS ** -0.5)
    w_all = jnp.concatenate([w[:, h:h + 1] for h in range(IDX_HEADS)], axis=0)
    qpos = past + lax.broadcasted_iota(I32, (t, LANE), 0)

    def index_scores(kmat_bf):
        d = lax.dot_general(qi_all, kmat_bf, (((1,), (1,)), ((), ())), preferred_element_type=F32)
        r = jnp.maximum(d, 0.0) * w_all
        return jnp.sum(r.reshape(IDX_HEADS, t, r.shape[1]), axis=0)

    start_chunk(ckidx_ref, ibuf, isem, 0, 0)
    for ch in range(nch):
        slot = ch % 2
        if ch + 1 < nch:
            start_chunk(ckidx_ref, ibuf, isem, ch + 1, 1 - slot)
        wait_chunk(ckidx_ref, ibuf, isem, ch, slot)
        kmat = ibuf[slot].reshape(kc, IDX_DIM).astype(BF)
        keys_ref[:, ch * kc:(ch + 1) * kc] = _score_keys(index_scores(kmat))
    knew = _pad_rows(misc[:, :IDX_DIM], LANE).astype(BF)
    new_pos = past + lax.broadcasted_iota(I32, (t, LANE), 1)
    keys_ref[:, past:past + LANE] = jnp.where(new_pos <= qpos, _score_keys(index_scores(knew)), INT_MIN)

    width = past + LANE
    col = lax.broadcasted_iota(I32, (t, width), 1)

    def count_where(indicator):
        return jnp.sum(_lane_fold(indicator(keys_ref[...])), axis=1, keepdims=True)

    thr = _kth_largest_key(lambda cand: count_where(lambda k: jnp.where(k >= cand, 1, 0)), t, topk)
    n_ge = count_where(lambda k: jnp.where(k >= thr, 1, 0))
    surplus = jnp.where(n_ge > topk, jnp.where(thr != INT_MIN, 1, 0), 0)

    @pl.when(jnp.max(surplus) > 0)
    def _():
        need = topk - count_where(lambda k: jnp.where(k > thr, 1, 0))
        nbits = max(1, (width - 1).bit_length())
        cut = _tie_cut(lambda cand: count_where(
            lambda k: jnp.where(k == thr, jnp.where(col < cand, 1, 0), 0)), need, t, nbits)
        k = keys_ref[...]
        keys_ref[...] = k - jnp.where(k == thr, jnp.where(col > cut, surplus, 0), 0)

    thr = jnp.maximum(thr, INT_MIN + 1)

    q_all = jnp.concatenate([qa_ref[0, :, h * LANE:(h + 1) * LANE] for h in range(N_HEADS)],
                            axis=0).astype(BF)

    def attend(carry, kmat_bf, vmat_bf, sel):
        m_old, l_old, acc_old = carry
        s = lax.dot_general(q_all, kmat_bf, (((1,), (1,)), ((), ())), preferred_element_type=F32)
        n = s.shape[1]
        s = jnp.where(sel[None], s.reshape(N_HEADS, t, n), -jnp.inf).reshape(N_HEADS * t, n)
        m_new = jnp.maximum(m_old, jnp.max(s, axis=1, keepdims=True))
        alpha = jnp.exp(m_old - m_new)
        p = jnp.exp(s - m_new)
        l_new = alpha * l_old + jnp.sum(p, axis=1, keepdims=True)
        acc_new = alpha * acc_old + jnp.dot(p.astype(BF), vmat_bf, preferred_element_type=F32)
        return m_new, l_new, acc_new

    carry = (jnp.full((hq, 1), NEG_BIG, F32), jnp.zeros((hq, 1), F32), jnp.zeros((hq, LANE), F32))
    start_chunk(ck_ref, kbuf, ksem, 0, 0)
    start_chunk(cv_ref, vbuf, vsem, 0, 0)
    for ch in range(nch):
        slot = ch % 2
        if ch + 1 < nch:
            start_chunk(ck_ref, kbuf, ksem, ch + 1, 1 - slot)
            start_chunk(cv_ref, vbuf, vsem, ch + 1, 1 - slot)
        wait_chunk(ck_ref, kbuf, ksem, ch, slot)
        wait_chunk(cv_ref, vbuf, vsem, ch, slot)
        sel = keys_ref[:, ch * kc:(ch + 1) * kc] >= thr
        carry = attend(carry, kbuf[slot].reshape(kc, LANE).astype(BF), vbuf[slot].reshape(kc, LANE).astype(BF), sel)
    sel_new = keys_ref[:, past:past + LANE] >= thr
    carry = attend(carry, _pad_rows(ka_ref[0], LANE).astype(BF), _pad_rows(va_ref[0], LANE).astype(BF), sel_new)
    _, l_fin, acc_fin = carry
    o = acc_fin / l_fin
    for h in range(N_HEADS):
        g = h // GQA_GROUP
        out_ref[0, :, h * HEAD_DIM:(h + 1) * HEAD_DIM] = (
            o[h * t:(h + 1) * t, g * HEAD_DIM:(g + 1) * HEAD_DIM].astype(out_ref.dtype))


def _dsa_sample(page_table, qi, misc, qa, ka, va, cache_kidx, cache_k, cache_v, ppc):
    batch, t, _ = misc.shape
    n_pages = page_table.shape[1]
    past = n_pages * PAGE_SIZE
    topk = min(TOPK_MAX, (past + t) // 4)
    kern = functools.partial(_dsa_sample_kernel, t=t, n_pages=n_pages, ppc=ppc, topk=topk)
    row = lambda width: pl.BlockSpec((1, t, width), lambda b, pt: (b, 0, 0))
    hbm = pl.BlockSpec(memory_space=pl.ANY)
    grid_spec = pltpu.PrefetchScalarGridSpec(
        num_scalar_prefetch=1,
        grid=(batch,),
        in_specs=[row(IDX_HEADS * LANE), row(LANE), row(N_HEADS * LANE), row(LANE), row(LANE),
                  hbm, hbm, hbm],
        out_specs=row(N_HEADS * HEAD_DIM),
        scratch_shapes=[pltpu.VMEM((2, ppc, PAGE_SIZE, IDX_DIM), F32),
                        pltpu.VMEM((2, ppc, PAGE_SIZE, LANE), F32),
                        pltpu.VMEM((2, ppc, PAGE_SIZE, LANE), F32),
                        pltpu.SemaphoreType.DMA((2,)), pltpu.SemaphoreType.DMA((2,)), pltpu.SemaphoreType.DMA((2,)),
                        pltpu.VMEM((t, past + LANE), I32)],
    )
    return pl.pallas_call(
        kern,
        grid_spec=grid_spec,
        out_shape=jax.ShapeDtypeStruct((batch, t, N_HEADS * HEAD_DIM), F32),
        compiler_params=pltpu.CompilerParams(dimension_semantics=("arbitrary",), vmem_limit_bytes=VMEM_LIMIT),
        name="dsa_sample",
    )(page_table, qi, misc, qa, ka, va, cache_kidx, cache_k, cache_v)


def _row_tile(m, cap):
    tm = min(m, cap)
    while m % tm:
        tm //= 2
    return tm


def _mixer(x2d, weights, batch, seq, attend, gla_s0, gla_chunk, gla_nchunks, gla_out_dtype):
    (w_packed, wal, bal, gn, wao, wgo, wout, ln1g, ln1b, wf1, wf2, ln2g, ln2b) = weights
    tm = _row_tile(x2d.shape[0], 512)
    (qa, ka, va, kab, vab, qi, misc, miscb, qb, kb, vb, gb, ga, gb2) = _proj(x2d, w_packed, tm)
    attn = attend(qi, misc, miscb, qa, ka, va, kab, vab)
    b3 = lambda a: a.reshape(batch, seq, a.shape[-1])
    ob, s_new = _gla(b3(misc), wal, bal, b3(qb), b3(kb), b3(vb), b3(gb), gn, gla_s0,
                     gla_chunk, gla_nchunks, gla_out_dtype)
    ob = ob.reshape(batch * seq, GLA_DV_TOT).astype(BF)
    y = _post(x2d, attn, ob, ga, gb2, wao, wgo, wout, ln1g, ln1b, wf1, wf2, ln2g, ln2b, tm)
    return y, ka, va, misc[:, :IDX_DIM], s_new


def kernel(x_prompt, x_sample, cache_k, cache_v, cache_kidx, state_gla, page_table,
           w_in, w_alpha2, b_alpha, gla_norm_g, w_attn_o, w_gla_o, w_out,
           ln1_g, ln1_b, w_ff1, w_ff2, ln2_g, ln2_b):
    assert w_in.shape[0] == DEPTH == 1
    bp, tp, _ = x_prompt.shape
    bs, ts, _ = x_sample.shape
    n_pool = cache_k.shape[1]
    l = 0
    wal = jnp.zeros((LANE, GLA_DK_TOT), F32).at[MISC_AB:MISC_AB + GATE_RANK].set(w_alpha2[l])
    wal = wal.reshape(LANE, GLA_HEADS, GLA_DK).transpose(1, 0, 2).astype(BF)
    weights = (_pack_w_in(w_in[l]), wal, b_alpha[l].reshape(GLA_HEADS, 1, GLA_DK), gla_norm_g[l].reshape(1, GLA_DV),
               w_attn_o[l].astype(BF), w_gla_o[l].astype(BF), w_out[l].astype(BF),
               ln1_g[l].reshape(1, D_MODEL), ln1_b[l].reshape(1, D_MODEL),
               w_ff1[l].astype(BF), w_ff2[l].astype(BF), ln2_g[l].reshape(1, D_MODEL), ln2_b[l].reshape(1, D_MODEL))

    tq = min(128, tp)
    kc = min(512, tp)
    def attend_p(qi, misc, miscb, qa, ka, va, kab, vab):
        return _dsa_prompt(qi, misc, miscb, qa, kab, vab, batch=bp, seq=tp, tq=tq, kc=kc)

    gchunk = min(GLA_CHUNK, tp)
    y_p, k_p, v_p, ki_p, s_p = _mixer(x_prompt.reshape(bp * tp, D_MODEL), weights, bp, tp, attend_p,
                                      None, gchunk, min(8, tp // gchunk), BF)

    ck = cache_k[l].reshape(n_pool, PAGE_SIZE, N_KV_HEADS * HEAD_DIM)
    cv = cache_v[l].reshape(n_pool, PAGE_SIZE, N_KV_HEADS * HEAD_DIM)

    def attend_s(qi, misc, miscb, qa, ka, va, kab, vab):
        f3 = lambda a: a.astype(F32).reshape(bs, ts, a.shape[-1])
        o = _dsa_sample(page_table, f3(qi), f3(misc), f3(qa), f3(ka), f3(va), cache_kidx[l], ck, cv,
                        ppc=min(16, page_table.shape[1]))
        return o.reshape(bs * ts, N_HEADS * HEAD_DIM).astype(BF)

    y_s, k_s, v_s, ki_s, s_s = _mixer(x_sample.reshape(bs * ts, D_MODEL), weights, bs, ts, attend_s,
                                      state_gla[l], ts, 1, F32)

    def kv(a, b, t):
        return a.reshape(1, b, t, N_KV_HEADS, HEAD_DIM)

    return (y_p.reshape(bp, tp, D_MODEL), y_s.reshape(bs, ts, D_MODEL),
            kv(k_p, bp, tp), kv(v_p, bp, tp), ki_p.reshape(1, bp, tp, IDX_DIM), s_p[None],
            kv(k_s, bs, ts), kv(v_s, bs, ts), ki_s.reshape(1, bs, ts, IDX_DIM), s_s[None])
```

```python
import functools

import jax
import jax.numpy as jnp
from jax import lax
from jax.experimental import pallas as pl
from jax.experimental.pallas import tpu as pltpu

D_MODEL = 1024
N_HEADS = 8
N_KV_HEADS = 2
HEAD_DIM = 64
GQA_GROUP = N_HEADS // N_KV_HEADS
IDX_HEADS = 8
IDX_DIM = 64
TOPK_MAX = 256
PAGE_SIZE = 128
GLA_HEADS = 4
GLA_DK_TOT = D_MODEL // 2
GLA_DV_TOT = D_MODEL
GLA_DK = GLA_DK_TOT // GLA_HEADS
GLA_DV = GLA_DV_TOT // GLA_HEADS
GATE_RANK = 16
GATE_TAU = 16.0
GLA_CHUNK = 64
D_FF = 4 * D_MODEL
NORM_EPS = 1e-5
DEPTH = 1
DN_ALPHA = (2 * DEPTH) ** 0.25

LANE = 128
SUBLANE = 8
VMEM_LIMIT = 56 * 1024 * 1024
BF = jnp.bfloat16
F32 = jnp.float32
I32 = jnp.int32
I16 = jnp.int16
INT_MIN = -2 ** 31
I16_MIN = -2 ** 15
NEG_BIG = -1e30

MISC_WI = IDX_DIM
MISC_AB = IDX_DIM + IDX_HEADS

_PROJ_GROUPS = (
    ('qa', N_HEADS * LANE), ('ka', LANE), ('va', LANE), ('qi', IDX_HEADS * LANE), ('misc', LANE),
    ('qb', GLA_DK_TOT), ('kb', GLA_DK_TOT), ('vb', GLA_DV_TOT), ('gb', GLA_DV_TOT),
    ('ga', D_MODEL), ('gb2', D_MODEL),
)
_PROJ_OFF = {}
_o = 0
for _n, _w in _PROJ_GROUPS:
    _PROJ_OFF[_n] = (_o, _o + _w)
    _o += _w
PROJ_COLS = _o


def _const_spec(shape):
    nd = len(shape)
    return pl.BlockSpec(shape, lambda *_: (0,) * nd, pipeline_mode=pl.Buffered(1))


def _pack_w_in(w_in):
    off = 0
    parts = {}
    for name, width in (('q_a', 512), ('k_a', 128), ('v_a', 128), ('q_i', 512), ('k_i', 64), ('w_i', 8),
                        ('q_b', 512), ('k_b', 512), ('v_b', 1024), ('g_b', 1024), ('a_b', 16),
                        ('gate_a', 1024), ('gate_b', 1024)):
        parts[name] = w_in[:, off:off + width]
        off += width
    z64 = jnp.zeros((D_MODEL, HEAD_DIM), F32)
    qa_cols = []
    for h in range(N_HEADS):
        q = parts['q_a'][:, h * HEAD_DIM:(h + 1) * HEAD_DIM] * (HEAD_DIM ** -0.5)
        qa_cols += [q, z64] if h // GQA_GROUP == 0 else [z64, q]
    qi_cols = []
    for h in range(IDX_HEADS):
        qi_cols += [parts['q_i'][:, h * IDX_DIM:(h + 1) * IDX_DIM] * (IDX_DIM ** -0.5), z64]
    misc = jnp.concatenate([parts['k_i'], parts['w_i'], parts['a_b'],
                            jnp.zeros((D_MODEL, LANE - MISC_AB - GATE_RANK), F32)], axis=1)
    cols = qa_cols + [parts['k_a'], parts['v_a']] + qi_cols + [
        misc, parts['q_b'], parts['k_b'], parts['v_b'], parts['g_b'], parts['gate_a'], parts['gate_b']]
    return jnp.concatenate(cols, axis=1).astype(BF)


def _proj_kernel(x_ref, w_ref, qa_ref, ka_ref, va_ref, kab_ref, vab_ref, qi_ref, misc_ref, miscb_ref,
                 qb_ref, kb_ref, vb_ref, gb_ref, ga_ref, gb2_ref):
    xb = x_ref[...].astype(BF)

    def mm(name):
        a, b = _PROJ_OFF[name]
        return jnp.dot(xb, w_ref[:, a:b], preferred_element_type=F32)

    qa_ref[...] = mm('qa').astype(BF)
    ka = mm('ka')
    ka_ref[...] = ka
    kab_ref[...] = ka.astype(BF)
    va = mm('va')
    va_ref[...] = va
    vab_ref[...] = va.astype(BF)
    qi_ref[...] = mm('qi').astype(BF)
    misc = mm('misc')
    misc_ref[...] = misc
    miscb_ref[...] = misc.astype(BF)
    qb_ref[...] = mm('qb')
    kb_ref[...] = mm('kb')
    vb_ref[...] = mm('vb')
    gb_ref[...] = mm('gb')
    ga_ref[...] = mm('ga')
    gb2_ref[...] = mm('gb2')


def _proj(x2d, w_packed, tm):
    m = x2d.shape[0]
    outs = (('qa', BF), ('ka', F32), ('va', F32), ('ka', BF), ('va', BF), ('qi', BF), ('misc', F32), ('misc', BF),
            ('qb', F32), ('kb', F32), ('vb', F32), ('gb', F32), ('ga', F32), ('gb2', F32))
    widths = dict(_PROJ_GROUPS)
    return pl.pallas_call(
        _proj_kernel,
        grid=(m // tm,),
        in_specs=[pl.BlockSpec((tm, D_MODEL), lambda i: (i, 0)), _const_spec((D_MODEL, PROJ_COLS))],
        out_specs=[pl.BlockSpec((tm, widths[n]), lambda i: (i, 0)) for n, _ in outs],
        out_shape=[jax.ShapeDtypeStruct((m, widths[n]), dt) for n, dt in outs],
        compiler_params=pltpu.CompilerParams(dimension_semantics=("parallel",), vmem_limit_bytes=VMEM_LIMIT),
        name="proj",
    )(x2d, w_packed)


def _score_keys(scores):
    bits = pltpu.bitcast(scores, I32)
    return jnp.where(bits < 0, INT_MIN - bits, bits)


def _lane_fold(x, op=jnp.add):
    acc = x[:, :LANE]
    for j in range(1, x.shape[1] // LANE):
        acc = op(acc, x[:, j * LANE:(j + 1) * LANE])
    return acc


def _kth_largest_half(count_ge, rows, k):
    def body(it, ans):
        cand = ans + jnp.left_shift(jnp.int32(1), 15 - it)
        return jnp.where(count_ge(cand.astype(I16)) >= k, cand, ans)
    return lax.fori_loop(0, 16, body, jnp.full((rows, 1), I16_MIN, I32))


def _kth_largest_key(count_ge, rows, topk):
    def body(it, ans):
        cand = ans + jnp.left_shift(jnp.int32(1), 31 - it)
        return jnp.where(count_ge(cand) >= topk, cand, ans)
    return lax.fori_loop(0, 32, body, jnp.full((rows, 1), INT_MIN, I32))


def _tie_cut(count_tie_below, need, rows, nbits):
    def body(it, m):
        cand = m | jnp.left_shift(jnp.int32(1), nbits - 1 - it)
        return jnp.where(count_tie_below(cand) < need, cand, m)
    return lax.fori_loop(0, nbits, body, jnp.zeros((rows, 1), I32))


def _dsa_prompt_kernel(qi_ref, misc_ref, kib_ref, qa_ref, kab_ref, vab_ref, out_ref,
                       keys_ref, hi_ref, lo_ref, s_ref, thr_ref, *, tq, ts, kc, topk, seq):
    i = pl.program_id(1)
    nkc = ((i + 1) * tq + kc - 1) // kc
    lane_pos = lax.broadcasted_iota(I32, (tq, kc), 1)

    def score_chunk(c, carry):
        kchunk = kib_ref[pl.ds(pl.multiple_of(c * kc, kc), kc), :]
        for r in range(tq // ts):
            rows = slice(r * ts, (r + 1) * ts)
            w = misc_ref[rows, MISC_WI:MISC_WI + IDX_HEADS] * (IDX_HEADS ** -0.5)
            acc = jnp.zeros((ts, kc), F32)
            for h in range(IDX_HEADS):
                d = lax.dot_general(qi_ref[rows, h * LANE:(h + 1) * LANE], kchunk, (((1,), (1,)), ((), ())),
                                    preferred_element_type=F32)
                acc = acc + jnp.maximum(d, 0.0) * w[:, h:h + 1]
            tpos = i * tq + r * ts + lax.broadcasted_iota(I32, (ts, kc), 0)
            key = jnp.where(c * kc + lax.broadcasted_iota(I32, (ts, kc), 1) <= tpos, _score_keys(acc), INT_MIN)
            keys_ref[c, rows, :] = key
            hi_ref[c, rows, :] = (key >> 16).astype(I16)
            lo_ref[c, rows, :] = ((key & 0xFFFF) + I16_MIN).astype(I16)
        return carry
    lax.fori_loop(0, nkc, score_chunk, 0)

    def count_where(indicator):
        def body(c, cnt):
            return cnt + _lane_fold(indicator(c, keys_ref[c]))
        cnt = lax.fori_loop(0, nkc, body, jnp.zeros((tq, LANE), F32))
        return jnp.sum(cnt, axis=1, keepdims=True)

    def count16(ref, indicator):
        def body(c, cnt):
            return cnt + _lane_fold(indicator(ref[c]))
        cnt = lax.fori_loop(0, nkc, body, jnp.zeros((tq, LANE), I16))
        return jnp.sum(cnt.astype(F32), axis=1, keepdims=True)

    one16, zero16 = jnp.int16(1), jnp.int16(0)
    thr_hi = _kth_largest_half(lambda cand: count16(hi_ref, lambda v: jnp.where(v >= cand, one16, zero16)),
                               tq, float(topk))
    thr_hi16 = thr_hi.astype(I16)
    need_lo = float(topk) - count16(hi_ref, lambda v: jnp.where(v > thr_hi16, one16, zero16))

    def keep_bucket(c, carry):
        lo_ref[c] = jnp.where(hi_ref[c] == thr_hi16, lo_ref[c], jnp.int16(I16_MIN))
        return carry
    lax.fori_loop(0, nkc, keep_bucket, 0)
    thr_lo = _kth_largest_half(lambda cand: count16(lo_ref, lambda v: jnp.where(v >= cand, one16, zero16)),
                               tq, need_lo)
    thr = jnp.left_shift(thr_hi, 16) | (thr_lo - I16_MIN)
    n_ge = count_where(lambda c, k: jnp.where(k >= thr, 1.0, 0.0))
    surplus = jnp.where(n_ge > topk, jnp.where(thr != INT_MIN, 1, 0), 0)

    @pl.when(jnp.max(surplus) > 0)
    def _():
        need = topk - count_where(lambda c, k: jnp.where(k > thr, 1.0, 0.0))
        nbits = max(1, (seq - 1).bit_length())
        cut = _tie_cut(lambda cand: count_where(
            lambda c, k: jnp.where(k == thr, jnp.where(c * kc + lane_pos < cand, 1.0, 0.0), 0.0)), need, tq, nbits)

        def drop(c, carry):
            k = keys_ref[c]
            keys_ref[c] = k - jnp.where(k == thr, jnp.where(c * kc + lane_pos > cut, surplus, 0), 0)
            return carry
        lax.fori_loop(0, nkc, drop, 0)

    thr = jnp.maximum(thr, INT_MIN + 1)

    v_lanes = lax.broadcasted_iota(I32, (kc, LANE), 1) // HEAD_DIM
    thr_ref[...] = jnp.broadcast_to(thr, (tq, LANE))
    for r in range(tq // ts):
        rows = slice(r * ts, (r + 1) * ts)
        nkc_r = (i * tq + (r + 1) * ts + kc - 1) // kc
        thr_r = thr_ref[rows, 0:1]
        qg = [jnp.concatenate([qa_ref[rows, (g * GQA_GROUP + j) * LANE:(g * GQA_GROUP + j + 1) * LANE]
                               for j in range(GQA_GROUP)], axis=0) for g in range(N_KV_HEADS)]

        def score_pass(c, mrun):
            kchunk = kab_ref[pl.ds(pl.multiple_of(c * kc, kc), kc), :]
            sel = keys_ref[c, rows, :] >= thr_r
            out = []
            for g in range(N_KV_HEADS):
                s = lax.dot_general(qg[g], kchunk, (((1,), (1,)), ((), ())), preferred_element_type=F32)
                s = jnp.where(sel[None], s.reshape(GQA_GROUP, ts, kc), -jnp.inf).reshape(GQA_GROUP * ts, kc)
                s_ref[c, g] = s
                out.append(jnp.maximum(mrun[g], _lane_fold(s, jnp.maximum)))
            return tuple(out)
        mrun = lax.fori_loop(0, nkc_r, score_pass,
                             tuple(jnp.full((GQA_GROUP * ts, LANE), -jnp.inf, F32) for _ in range(N_KV_HEADS)))
        m = [jnp.max(mrun[g], axis=1, keepdims=True) for g in range(N_KV_HEADS)]

        def value_pass(c, acc):
            vchunk = vab_ref[pl.ds(pl.multiple_of(c * kc, kc), kc), :]
            out = []
            for g in range(N_KV_HEADS):
                vaug = jnp.where(v_lanes == g, vchunk, jnp.ones_like(vchunk))
                p = jnp.exp(s_ref[c, g] - m[g]).astype(BF)
                out.append(acc[g] + jnp.dot(p, vaug, preferred_element_type=F32))
            return tuple(out)
        acc = lax.fori_loop(0, nkc_r, value_pass,
                            tuple(jnp.zeros((GQA_GROUP * ts, LANE), F32) for _ in range(N_KV_HEADS)))

        for g in range(N_KV_HEADS):
            ones_lane = (1 - g) * HEAD_DIM
            o = acc[g] / acc[g][:, ones_lane:ones_lane + 1]
            for j in range(GQA_GROUP):
                h = g * GQA_GROUP + j
                out_ref[rows, h * HEAD_DIM:(h + 1) * HEAD_DIM] = (
                    o[j * ts:(j + 1) * ts, g * HEAD_DIM:(g + 1) * HEAD_DIM].astype(out_ref.dtype))


def _dsa_prompt(qi, misc, miscb, qa, kab, vab, batch, seq, tq, ts, kc):
    topk = min(TOPK_MAX, seq // 4)
    nq = seq // tq
    kern = functools.partial(_dsa_prompt_kernel, tq=tq, ts=ts, kc=kc, topk=topk, seq=seq)
    qrow = lambda b, i: (b * nq + i, 0)
    brow = lambda b, i: (b, 0)
    return pl.pallas_call(
        kern,
        grid=(batch, nq),
        in_specs=[pl.BlockSpec((tq, IDX_HEADS * LANE), qrow), pl.BlockSpec((tq, LANE), qrow),
                  pl.BlockSpec((seq, LANE), brow), pl.BlockSpec((tq, N_HEADS * LANE), qrow),
                  pl.BlockSpec((seq, LANE), brow), pl.BlockSpec((seq, LANE), brow)],
        out_specs=pl.BlockSpec((tq, N_HEADS * HEAD_DIM), qrow),
        out_shape=jax.ShapeDtypeStruct((batch * seq, N_HEADS * HEAD_DIM), BF),
        scratch_shapes=[pltpu.VMEM((seq // kc, tq, kc), I32),
                        pltpu.VMEM((seq // kc, tq, kc), I16),
                        pltpu.VMEM((seq // kc, tq, kc), I16),
                        pltpu.VMEM((seq // kc, N_KV_HEADS, GQA_GROUP * ts, kc), F32),
                        pltpu.VMEM((tq, LANE), I32)],
        compiler_params=pltpu.CompilerParams(dimension_semantics=("parallel", "arbitrary"),
                                             vmem_limit_bytes=VMEM_LIMIT),
        name="dsa_prompt",
    )(qi, misc, miscb, qa, kab, vab)


def _log_sigmoid(x):
    return jnp.minimum(x, 0.0) - jnp.log1p(jnp.exp(-jnp.abs(x)))


def _pad_rows(x, n):
    if x.shape[0] == n:
        return x
    return jnp.concatenate([x, jnp.zeros((n - x.shape[0], x.shape[1]), x.dtype)], axis=0)


def _gla_kernel(*refs, chunk, nchunks, has_s0):
    if has_s0:
        (misc_ref, wal_ref, bal_ref, q_ref, k_ref, v_ref, g_ref, gn_ref, s0_ref, o_ref, sout_ref, s_ref) = refs
    else:
        (misc_ref, wal_ref, bal_ref, q_ref, k_ref, v_ref, g_ref, gn_ref, o_ref, sout_ref, s_ref) = refs
        s0_ref = None
    t = pl.program_id(2)
    rows = chunk * nchunks
    cpad = max(chunk, 2 * SUBLANE)
    rpad = max(rows, 2 * SUBLANE)

    @pl.when(t == 0)
    def _():
        s_ref[...] = jnp.zeros(s_ref.shape, F32) if s0_ref is None else s0_ref[0, 0]

    la = jnp.dot(_pad_rows(misc_ref[0], rpad).astype(BF), wal_ref[0], preferred_element_type=F32)[:rows]
    la = _log_sigmoid(la + bal_ref[0]) / GATE_TAU
    pos = lax.broadcasted_iota(I32, (rows, GLA_DK), 0) % chunk
    cum = la
    d = 1
    while d < chunk:
        cum = cum + jnp.where(pos >= d, pltpu.roll(cum, d, axis=0), 0.0)
        d *= 2

    causal = (lax.broadcasted_iota(I32, (cpad, cpad), 0) >= lax.broadcasted_iota(I32, (cpad, cpad), 1))
    gn = gn_ref[...]
    for c in range(nchunks):
        r = slice(c * chunk, (c + 1) * chunk)
        cum_c = cum[r]
        last = cum_c[chunk - 1:chunk]
        q_c = q_ref[0, r, :] * (GLA_DK ** -0.5)
        k_c = k_ref[0, r, :]
        v_b = _pad_rows(v_ref[0, r, :], cpad).astype(BF)
        qt = _pad_rows(q_c * jnp.exp(cum_c), cpad).astype(BF)
        kt = _pad_rows(k_c * jnp.exp(-cum_c), cpad).astype(BF)
        kd = _pad_rows(k_c * jnp.exp(last - cum_c), cpad).astype(BF)
        att = lax.dot_general(qt, kt, (((1,), (1,)), ((), ())), preferred_element_type=F32)
        att = jnp.where(causal, att, 0.0).astype(BF)
        s_old = s_ref[...]
        o = (jnp.dot(att, v_b, preferred_element_type=F32)
             + jnp.dot(qt, s_old.astype(BF), preferred_element_type=F32))[:chunk]
        decay = jnp.transpose(jnp.broadcast_to(jnp.exp(last), (GLA_DK, GLA_DK)))
        upd = lax.dot_general(kd, v_b, (((0,), (0,)), ((), ())), preferred_element_type=F32)
        s_ref[...] = jnp.concatenate([decay] * (GLA_DV // GLA_DK), axis=1) * s_old + upd
        o = o * lax.rsqrt(jnp.mean(o * o, axis=-1, keepdims=True) + NORM_EPS) * gn
        gate = g_ref[0, r, :]
        o_ref[0, r, :] = (o * (gate * jax.nn.sigmoid(gate))).astype(o_ref.dtype)

    @pl.when(t == pl.num_programs(2) - 1)
    def _():
        sout_ref[0, 0] = s_ref[...]


def _gla(misc, wal, bal, qb, kb, vb, gb, gn, s0, chunk, nchunks, out_dtype):
    batch, seq, _ = misc.shape
    rows = chunk * nchunks
    head = lambda b, h, t: (b, t, h)
    in_specs = [pl.BlockSpec((1, rows, LANE), lambda b, h, t: (b, t, 0)),
                pl.BlockSpec((1, LANE, GLA_DK), lambda b, h, t: (h, 0, 0)),
                pl.BlockSpec((1, 1, GLA_DK), lambda b, h, t: (h, 0, 0)),
                pl.BlockSpec((1, rows, GLA_DK), head), pl.BlockSpec((1, rows, GLA_DK), head),
                pl.BlockSpec((1, rows, GLA_DV), head), pl.BlockSpec((1, rows, GLA_DV), head),
                pl.BlockSpec((1, GLA_DV), lambda b, h, t: (0, 0))]
    args = [misc, wal, bal, qb, kb, vb, gb, gn]
    state_spec = pl.BlockSpec((1, 1, GLA_DK, GLA_DV), lambda b, h, t: (b, h, 0, 0))
    if s0 is not None:
        in_specs.append(state_spec)
        args.append(s0)
    kern = functools.partial(_gla_kernel, chunk=chunk, nchunks=nchunks, has_s0=s0 is not None)
    return pl.pallas_call(
        kern,
        grid=(batch, GLA_HEADS, seq // rows),
        in_specs=in_specs,
        out_specs=[pl.BlockSpec((1, rows, GLA_DV), head), state_spec],
        out_shape=[jax.ShapeDtypeStruct((batch, seq, GLA_DV_TOT), out_dtype),
                   jax.ShapeDtypeStruct((batch, GLA_HEADS, GLA_DK, GLA_DV), F32)],
        scratch_shapes=[pltpu.VMEM((GLA_DK, GLA_DV), F32)],
        compiler_params=pltpu.CompilerParams(dimension_semantics=("parallel", "parallel", "arbitrary"),
                                             vmem_limit_bytes=VMEM_LIMIT),
        name="gla",
    )(*args)


def _layer_norm(x, g, b):
    xc = x - jnp.mean(x, axis=-1, keepdims=True)
    var = jnp.mean(xc * xc, axis=-1, keepdims=True)
    return xc * lax.rsqrt(var + NORM_EPS) * g + b


def _post_kernel(x_ref, attn_ref, ob_ref, ga_ref, gb_ref, wao_ref, wgo_ref, wout_ref, ln1g_ref, ln1b_ref,
                 wf1_ref, wf2_ref, ln2g_ref, ln2b_ref, out_ref, *, ff_chunk):
    branch_a = jnp.dot(attn_ref[...], wao_ref[...], preferred_element_type=F32)
    branch_b = jnp.dot(ob_ref[...], wgo_ref[...], preferred_element_type=F32)
    merged = jax.nn.sigmoid(ga_ref[...]) * branch_a + jax.nn.sigmoid(gb_ref[...]) * branch_b
    mix = jnp.dot(merged.astype(BF), wout_ref[...], preferred_element_type=F32)
    h = _layer_norm(DN_ALPHA * x_ref[...] + mix, ln1g_ref[...], ln1b_ref[...])
    hb = h.astype(BF)
    ff = jnp.zeros(h.shape, F32)
    for c in range(D_FF // ff_chunk):
        cols = slice(c * ff_chunk, (c + 1) * ff_chunk)
        a = jnp.maximum(jnp.dot(hb, wf1_ref[:, cols], preferred_element_type=F32), 0.0)
        ff = ff + jnp.dot((a * a).astype(BF), wf2_ref[cols, :], preferred_element_type=F32)
    out_ref[...] = _layer_norm(DN_ALPHA * h + ff, ln2g_ref[...], ln2b_ref[...])


def _post(x2d, attn, ob, ga, gb, wao, wgo, wout, ln1g, ln1b, wf1, wf2, ln2g, ln2b, tm):
    m = x2d.shape[0]
    row = lambda width: pl.BlockSpec((tm, width), lambda i: (i, 0))
    return pl.pallas_call(
        functools.partial(_post_kernel, ff_chunk=1024),
        grid=(m // tm,),
        in_specs=[row(D_MODEL), row(N_HEADS * HEAD_DIM), row(GLA_DV_TOT), row(D_MODEL), row(D_MODEL),
                  _const_spec(wao.shape), _const_spec(wgo.shape), _const_spec(wout.shape),
                  _const_spec(ln1g.shape), _const_spec(ln1b.shape), _const_spec(wf1.shape), _const_spec(wf2.shape),
                  _const_spec(ln2g.shape), _const_spec(ln2b.shape)],
        out_specs=row(D_MODEL),
        out_shape=jax.ShapeDtypeStruct((m, D_MODEL), F32),
        compiler_params=pltpu.CompilerParams(dimension_semantics=("parallel",), vmem_limit_bytes=VMEM_LIMIT),
        name="post",
    )(x2d, attn, ob, ga, gb, wao, wgo, wout, ln1g, ln1b, wf1, wf2, ln2g, ln2b)


def _dsa_sample_kernel(pt_ref, qi_ref, misc_ref, qa_ref, ka_ref, va_ref,
                       ckidx_ref, ck_ref, cv_ref, out_ref,
                       ibuf, kbuf, vbuf, isem, ksem, vsem, keys_ref, *, t, n_pages, ppc, topk):
    b = pl.program_id(0)
    nch = n_pages // ppc
    kc = ppc * PAGE_SIZE
    past = n_pages * PAGE_SIZE
    hq = IDX_HEADS * t

    def page_copy(src_ref, buf, sem, ch, slot, p):
        return pltpu.make_async_copy(src_ref.at[pt_ref[b, ch * ppc + p]], buf.at[slot, p], sem.at[slot])

    def start_chunk(src_ref, buf, sem, ch, slot):
        for p in range(ppc):
            page_copy(src_ref, buf, sem, ch, slot, p).start()

    def wait_chunk(src_ref, buf, sem, ch, slot):
        for p in range(ppc):
            page_copy(src_ref, buf, sem, ch, slot, p).wait()

    qi_all = jnp.concatenate([qi_ref[0, :, h * LANE:h * LANE + IDX_DIM] for h in range(IDX_HEADS)],
                             axis=0).astype(BF)
    misc = misc_ref[0]
    w = misc[:, MISC_WI:MISC_WI + IDX_HEADS] * (IDX_HEADS ** -0.5)
    w_all = jnp.concatenate([w[:, h:h + 1] for h in range(IDX_HEADS)], axis=0)
    qpos = past + lax.broadcasted_iota(I32, (t, LANE), 0)

    def index_scores(kmat_bf):
        d = lax.dot_general(qi_all, kmat_bf, (((1,), (1,)), ((), ())), preferred_element_type=F32)
        r = jnp.maximum(d, 0.0) * w_all
        return jnp.sum(r.reshape(IDX_HEADS, t, r.shape[1]), axis=0)

    start_chunk(ckidx_ref, ibuf, isem, 0, 0)
    for ch in range(nch):
        slot = ch % 2
        if ch + 1 < nch:
            start_chunk(ckidx_ref, ibuf, isem, ch + 1, 1 - slot)
        wait_chunk(ckidx_ref, ibuf, isem, ch, slot)
        kmat = ibuf[slot].reshape(kc, IDX_DIM).astype(BF)
        keys_ref[:, ch * kc:(ch + 1) * kc] = _score_keys(index_scores(kmat))
    knew = _pad_rows(misc[:, :IDX_DIM], LANE).astype(BF)
    new_pos = past + lax.broadcasted_iota(I32, (t, LANE), 1)
    keys_ref[:, past:past + LANE] = jnp.where(new_pos <= qpos, _score_keys(index_scores(knew)), INT_MIN)

    width = past + LANE
    col = lax.broadcasted_iota(I32, (t, width), 1)

    def count_where(indicator):
        return jnp.sum(_lane_fold(indicator(keys_ref[...])), axis=1, keepdims=True)

    thr = _kth_largest_key(lambda cand: count_where(lambda k: jnp.where(k >= cand, 1, 0)), t, topk)
    n_ge = count_where(lambda k: jnp.where(k >= thr, 1, 0))
    surplus = jnp.where(n_ge > topk, jnp.where(thr != INT_MIN, 1, 0), 0)

    @pl.when(jnp.max(surplus) > 0)
    def _():
        need = topk - count_where(lambda k: jnp.where(k > thr, 1, 0))
        nbits = max(1, (width - 1).bit_length())
        cut = _tie_cut(lambda cand: count_where(
            lambda k: jnp.where(k == thr, jnp.where(col < cand, 1, 0), 0)), need, t, nbits)
        k = keys_ref[...]
        keys_ref[...] = k - jnp.where(k == thr, jnp.where(col > cut, surplus, 0), 0)

    thr = jnp.maximum(thr, INT_MIN + 1)

    q_all = jnp.concatenate([qa_ref[0, :, h * LANE:(h + 1) * LANE] for h in range(N_HEADS)],
                            axis=0).astype(BF)

    def attend(carry, kmat_bf, vmat_bf, sel):
        m_old, l_old, acc_old = carry
        s = lax.dot_general(q_all, kmat_bf, (((1,), (1,)), ((), ())), preferred_element_type=F32)
        n = s.shape[1]
        s = jnp.where(sel[None], s.reshape(N_HEADS, t, n), -jnp.inf).reshape(N_HEADS * t, n)
        m_new = jnp.maximum(m_old, jnp.max(s, axis=1, keepdims=True))
        alpha = jnp.exp(m_old - m_new)
        p = jnp.exp(s - m_new)
        l_new = alpha * l_old + jnp.sum(p, axis=1, keepdims=True)
        acc_new = alpha * acc_old + jnp.dot(p.astype(BF), vmat_bf, preferred_element_type=F32)
        return m_new, l_new, acc_new

    carry = (jnp.full((hq, 1), NEG_BIG, F32), jnp.zeros((hq, 1), F32), jnp.zeros((hq, LANE), F32))
    start_chunk(ck_ref, kbuf, ksem, 0, 0)
    start_chunk(cv_ref, vbuf, vsem, 0, 0)
    for ch in range(nch):
        slot = ch % 2
        if ch + 1 < nch:
            start_chunk(ck_ref, kbuf, ksem, ch + 1, 1 - slot)
            start_chunk(cv_ref, vbuf, vsem, ch + 1, 1 - slot)
        wait_chunk(ck_ref, kbuf, ksem, ch, slot)
        wait_chunk(cv_ref, vbuf, vsem, ch, slot)
        sel = keys_ref[:, ch * kc:(ch + 1) * kc] >= thr
        carry = attend(carry, kbuf[slot].reshape(kc, LANE).astype(BF), vbuf[slot].reshape(kc, LANE).astype(BF), sel)
    sel_new = keys_ref[:, past:past + LANE] >= thr
    carry = attend(carry, _pad_rows(ka_ref[0], LANE).astype(BF), _pad_rows(va_ref[0], LANE).astype(BF), sel_new)
    _, l_fin, acc_fin = carry
    o = acc_fin / l_fin
    for h in range(N_HEADS):
        g = h // GQA_GROUP
        out_ref[0, :, h * HEAD_DIM:(h + 1) * HEAD_DIM] = (
            o[h * t:(h + 1) * t, g * HEAD_DIM:(g + 1) * HEAD_DIM].astype(out_ref.dtype))


def _dsa_sample(page_table, qi, misc, qa, ka, va, cache_kidx, cache_k, cache_v, ppc):
    batch, t, _ = misc.shape
    n_pages = page_table.shape[1]
    past = n_pages * PAGE_SIZE
    topk = min(TOPK_MAX, (past + t) // 4)
    kern = functools.partial(_dsa_sample_kernel, t=t, n_pages=n_pages, ppc=ppc, topk=topk)
    row = lambda width: pl.BlockSpec((1, t, width), lambda b, pt: (b, 0, 0))
    hbm = pl.BlockSpec(memory_space=pl.ANY)
    grid_spec = pltpu.PrefetchScalarGridSpec(
        num_scalar_prefetch=1,
        grid=(batch,),
        in_specs=[row(IDX_HEADS * LANE), row(LANE), row(N_HEADS * LANE), row(LANE), row(LANE),
                  hbm, hbm, hbm],
        out_specs=row(N_HEADS * HEAD_DIM),
        scratch_shapes=[pltpu.VMEM((2, ppc, PAGE_SIZE, IDX_DIM), F32),
                        pltpu.VMEM((2, ppc, PAGE_SIZE, LANE), F32),
                        pltpu.VMEM((2, ppc, PAGE_SIZE, LANE), F32),
                        pltpu.SemaphoreType.DMA((2,)), pltpu.SemaphoreType.DMA((2,)), pltpu.SemaphoreType.DMA((2,)),
                        pltpu.VMEM((t, past + LANE), I32)],
    )
    return pl.pallas_call(
        kern,
        grid_spec=grid_spec,
        out_shape=jax.ShapeDtypeStruct((batch, t, N_HEADS * HEAD_DIM), F32),
        compiler_params=pltpu.CompilerParams(dimension_semantics=("arbitrary",), vmem_limit_bytes=VMEM_LIMIT),
        name="dsa_sample",
    )(page_table, qi, misc, qa, ka, va, cache_kidx, cache_k, cache_v)


def _row_tile(m, cap):
    tm = min(m, cap)
    while m % tm:
        tm //= 2
    return tm


def _mixer(x2d, weights, batch, seq, attend, gla_s0, gla_chunk, gla_nchunks, gla_out_dtype):
    (w_packed, wal, bal, gn, wao, wgo, wout, ln1g, ln1b, wf1, wf2, ln2g, ln2b) = weights
    tm = _row_tile(x2d.shape[0], 512)
    (qa, ka, va, kab, vab, qi, misc, miscb, qb, kb, vb, gb, ga, gb2) = _proj(x2d, w_packed, tm)
    attn = attend(qi, misc, miscb, qa, ka, va, kab, vab)
    b3 = lambda a: a.reshape(batch, seq, a.shape[-1])
    ob, s_new = _gla(b3(misc), wal, bal, b3(qb), b3(kb), b3(vb), b3(gb), gn, gla_s0,
                     gla_chunk, gla_nchunks, gla_out_dtype)
    ob = ob.reshape(batch * seq, GLA_DV_TOT).astype(BF)
    y = _post(x2d, attn, ob, ga, gb2, wao, wgo, wout, ln1g, ln1b, wf1, wf2, ln2g, ln2b, tm)
    return y, ka, va, misc[:, :IDX_DIM], s_new


def kernel(x_prompt, x_sample, cache_k, cache_v, cache_kidx, state_gla, page_table,
           w_in, w_alpha2, b_alpha, gla_norm_g, w_attn_o, w_gla_o, w_out,
           ln1_g, ln1_b, w_ff1, w_ff2, ln2_g, ln2_b):
    assert w_in.shape[0] == DEPTH == 1
    bp, tp, _ = x_prompt.shape
    bs, ts, _ = x_sample.shape
    n_pool = cache_k.shape[1]
    l = 0
    wal = jnp.zeros((LANE, GLA_DK_TOT), F32).at[MISC_AB:MISC_AB + GATE_RANK].set(w_alpha2[l])
    wal = wal.reshape(LANE, GLA_HEADS, GLA_DK).transpose(1, 0, 2).astype(BF)
    weights = (_pack_w_in(w_in[l]), wal, b_alpha[l].reshape(GLA_HEADS, 1, GLA_DK), gla_norm_g[l].reshape(1, GLA_DV),
               w_attn_o[l].astype(BF), w_gla_o[l].astype(BF), w_out[l].astype(BF),
               ln1_g[l].reshape(1, D_MODEL), ln1_b[l].reshape(1, D_MODEL),
               w_ff1[l].astype(BF), w_ff2[l].astype(BF), ln2_g[l].reshape(1, D_MODEL), ln2_b[l].reshape(1, D_MODEL))

    tq = min(256, tp)
    kc = min(512, tp)

    def attend_p(qi, misc, miscb, qa, ka, va, kab, vab):
        return _dsa_prompt(qi, misc, miscb, qa, kab, vab, batch=bp, seq=tp, tq=tq, ts=min(128, tq), kc=kc)

    gchunk = min(GLA_CHUNK, tp)
    y_p, k_p, v_p, ki_p, s_p = _mixer(x_prompt.reshape(bp * tp, D_MODEL), weights, bp, tp, attend_p,
                                      None, gchunk, min(8, tp // gchunk), BF)

    ck = cache_k[l].reshape(n_pool, PAGE_SIZE, N_KV_HEADS * HEAD_DIM)
    cv = cache_v[l].reshape(n_pool, PAGE_SIZE, N_KV_HEADS * HEAD_DIM)

    def attend_s(qi, misc, miscb, qa, ka, va, kab, vab):
        f3 = lambda a: a.astype(F32).reshape(bs, ts, a.shape[-1])
        o = _dsa_sample(page_table, f3(qi), f3(misc), f3(qa), f3(ka), f3(va), cache_kidx[l], ck, cv,
                        ppc=min(16, page_table.shape[1]))
        return o.reshape(bs * ts, N_HEADS * HEAD_DIM).astype(BF)

    y_s, k_s, v_s, ki_s, s_s = _mixer(x_sample.reshape(bs * ts, D_MODEL), weights, bs, ts, attend_s,
                                      state_gla[l], ts, 1, F32)

    def kv(a, b, t):
        return a.reshape(1, b, t, N_KV_HEADS, HEAD_DIM)

    return (y_p.reshape(bp, tp, D_MODEL), y_s.reshape(bs, ts, D_MODEL),
            kv(k_p, bp, tp), kv(v_p, bp, tp), ki_p.reshape(1, bp, tp, IDX_DIM), s_p[None],
            kv(k_s, bs, ts), kv(v_s, bs, ts), ki_s.reshape(1, bs, ts, IDX_DIM), s_s[None])
```

```python
import functools

import jax
import jax.numpy as jnp
from jax import lax
from jax.experimental import pallas as pl
from jax.experimental.pallas import tpu as pltpu

D_MODEL = 1024
N_HEADS = 8
N_KV_HEADS = 2
HEAD_DIM = 64
GQA_GROUP = N_HEADS // N_KV_HEADS
IDX_HEADS = 8
IDX_DIM = 64
TOPK_MAX = 256
PAGE_SIZE = 128
GLA_HEADS = 4
GLA_DK_TOT = D_MODEL // 2
GLA_DV_TOT = D_MODEL
GLA_DK = GLA_DK_TOT // GLA_HEADS
GLA_DV = GLA_DV_TOT // GLA_HEADS
GATE_RANK = 16
GATE_TAU = 16.0
GLA_CHUNK = 64
D_FF = 4 * D_MODEL
NORM_EPS = 1e-5
DEPTH = 1
DN_ALPHA = (2 * DEPTH) ** 0.25

LANE = 128
SUBLANE = 8
VMEM_LIMIT = 56 * 1024 * 1024
BF = jnp.bfloat16
F32 = jnp.float32
I32 = jnp.int32
I16 = jnp.int16
INT_MIN = -2 ** 31
I16_MIN = -2 ** 15
NEG_BIG = -1e30

MISC_WI = IDX_DIM
MISC_AB = IDX_DIM + IDX_HEADS

_PROJ_GROUPS = (
    ('qa', N_HEADS * LANE), ('ka', LANE), ('va', LANE), ('qi', IDX_HEADS * LANE), ('misc', LANE),
    ('qb', GLA_DK_TOT), ('kb', GLA_DK_TOT), ('vb', GLA_DV_TOT), ('gb', GLA_DV_TOT),
    ('ga', D_MODEL), ('gb2', D_MODEL),
)
_PROJ_OFF = {}
_o = 0
for _n, _w in _PROJ_GROUPS:
    _PROJ_OFF[_n] = (_o, _o + _w)
    _o += _w
PROJ_COLS = _o


def _const_spec(shape):
    nd = len(shape)
    return pl.BlockSpec(shape, lambda *_: (0,) * nd, pipeline_mode=pl.Buffered(1))


def _pack_w_in(w_in):
    off = 0
    parts = {}
    for name, width in (('q_a', 512), ('k_a', 128), ('v_a', 128), ('q_i', 512), ('k_i', 64), ('w_i', 8),
                        ('q_b', 512), ('k_b', 512), ('v_b', 1024), ('g_b', 1024), ('a_b', 16),
                        ('gate_a', 1024), ('gate_b', 1024)):
        parts[name] = w_in[:, off:off + width]
        off += width
    z64 = jnp.zeros((D_MODEL, HEAD_DIM), F32)
    qa_cols = []
    for h in range(N_HEADS):
        q = parts['q_a'][:, h * HEAD_DIM:(h + 1) * HEAD_DIM] * (HEAD_DIM ** -0.5)
        qa_cols += [q, z64] if h // GQA_GROUP == 0 else [z64, q]
    qi_cols = []
    for h in range(IDX_HEADS):
        qi_cols += [parts['q_i'][:, h * IDX_DIM:(h + 1) * IDX_DIM] * (IDX_DIM ** -0.5), z64]
    misc = jnp.concatenate([parts['k_i'], parts['w_i'], parts['a_b'],
                            jnp.zeros((D_MODEL, LANE - MISC_AB - GATE_RANK), F32)], axis=1)
    cols = qa_cols + [parts['k_a'], parts['v_a']] + qi_cols + [
        misc, parts['q_b'], parts['k_b'], parts['v_b'], parts['g_b'], parts['gate_a'], parts['gate_b']]
    return jnp.concatenate(cols, axis=1).astype(BF)


def _proj_kernel(x_ref, w_ref, *out_refs, token_minor):
    xb = x_ref[...].astype(BF)

    def mm(name):
        a, b = _PROJ_OFF[name]
        return jnp.dot(xb, w_ref[:, a:b], preferred_element_type=F32)

    if token_minor:
        (qa_ref, kat_ref, vat_ref, kabt_ref, vab_ref, qi_ref, misc_ref, kit_ref, miscbt_ref,
         qb_ref, kb_ref, vb_ref, gb_ref, ga_ref, gb2_ref, ka_tmp, va_tmp) = out_refs
    else:
        (qa_ref, ka_ref, va_ref, qi_ref, misc_ref, qb_ref, kb_ref, vb_ref, gb_ref, ga_ref, gb2_ref) = out_refs
    qa_ref[...] = mm('qa').astype(BF)
    qi_ref[...] = mm('qi').astype(BF)
    ka, va, misc = mm('ka'), mm('va'), mm('misc')
    misc_ref[...] = misc
    if token_minor:
        ka_tmp[...] = ka
        va_tmp[...] = va
        ka_t, va_t, misc_t = ka_tmp[...].T, va_tmp[...].T, misc_ref[...].T
        kat_ref[0] = ka_t
        vat_ref[0] = va_t
        kit_ref[0] = misc_t[:IDX_DIM]
        kabt_ref[0] = ka_t.astype(BF)
        miscbt_ref[0] = misc_t.astype(BF)
        vab_ref[...] = va.astype(BF)
    else:
        ka_ref[...] = ka
        va_ref[...] = va
    qb_ref[...] = mm('qb')
    kb_ref[...] = mm('kb')
    vb_ref[...] = mm('vb')
    gb_ref[...] = mm('gb')
    ga_ref[...] = mm('ga')
    gb2_ref[...] = mm('gb2')


def _proj(x2d, w_packed, tm, token_minor_seq=None):
    m = x2d.shape[0]
    widths = dict(_PROJ_GROUPS)
    row = lambda n, dt: (pl.BlockSpec((tm, widths[n]), lambda i: (i, 0)), jax.ShapeDtypeStruct((m, widths[n]), dt))
    if token_minor_seq is None:
        outs = [row('qa', BF), row('ka', F32), row('va', F32), row('qi', BF), row('misc', F32)]
    else:
        nt = token_minor_seq // tm
        seq_t = lambda rows: (pl.BlockSpec((1, rows, tm), lambda i: (i // nt, 0, i % nt)),
                              jax.ShapeDtypeStruct((m // token_minor_seq, rows, token_minor_seq), F32))
        chunk_t = (pl.BlockSpec((1, LANE, tm), lambda i: (i, 0, 0)), jax.ShapeDtypeStruct((m // tm, LANE, tm), BF))
        outs = [row('qa', BF), seq_t(LANE), seq_t(LANE), chunk_t, row('va', BF), row('qi', BF), row('misc', F32),
                seq_t(IDX_DIM), chunk_t]
    outs += [row(n, F32) for n in ('qb', 'kb', 'vb', 'gb', 'ga', 'gb2')]
    return pl.pallas_call(
        functools.partial(_proj_kernel, token_minor=token_minor_seq is not None),
        grid=(m // tm,),
        in_specs=[pl.BlockSpec((tm, D_MODEL), lambda i: (i, 0)), _const_spec((D_MODEL, PROJ_COLS))],
        out_specs=[spec for spec, _ in outs],
        out_shape=[shape for _, shape in outs],
        scratch_shapes=[] if token_minor_seq is None else [pltpu.VMEM((tm, LANE), F32)] * 2,
        compiler_params=pltpu.CompilerParams(dimension_semantics=("parallel",), vmem_limit_bytes=VMEM_LIMIT),
        name="proj",
    )(x2d, w_packed)


def _score_keys(scores):
    bits = pltpu.bitcast(scores, I32)
    return jnp.where(bits < 0, INT_MIN - bits, bits)


def _lane_fold(x, op=jnp.add):
    acc = x[:, :LANE]
    for j in range(1, x.shape[1] // LANE):
        acc = op(acc, x[:, j * LANE:(j + 1) * LANE])
    return acc


def _kth_largest_half(count_ge, shape, k):
    def body(it, ans):
        cand = ans + jnp.left_shift(jnp.int32(1), 15 - it)
        return jnp.where(count_ge(cand) >= k, cand, ans)
    return lax.fori_loop(0, 16, body, jnp.full(shape, I16_MIN, I32))


def _kth_largest_key(count_ge, rows, topk):
    def body(it, ans):
        cand = ans + jnp.left_shift(jnp.int32(1), 31 - it)
        return jnp.where(count_ge(cand) >= topk, cand, ans)
    return lax.fori_loop(0, 32, body, jnp.full((rows, 1), INT_MIN, I32))


def _tie_cut(count_tie_below, need, rows, nbits):
    def body(it, m):
        cand = m | jnp.left_shift(jnp.int32(1), nbits - 1 - it)
        return jnp.where(count_tie_below(cand) < need, cand, m)
    return lax.fori_loop(0, nbits, body, jnp.zeros((rows, 1), I32))


def _dsa_prompt_kernel(qi_ref, misc_ref, kibt_ref, qa_ref, kabt_ref, vab_ref, out_ref,
                       keys_ref, hi_ref, lo_ref, s_ref, thr_ref, *, tq, ts, kc, topk, seq):
    i = pl.program_id(1)
    nkc = ((i + 1) * tq + kc - 1) // kc
    lane_pos = lax.broadcasted_iota(I32, (tq, kc), 1)

    def head_stack(ref, rows, nheads):
        return jnp.concatenate([ref[rows, h * LANE:(h + 1) * LANE] for h in range(nheads)], axis=0)

    qi_stack = [head_stack(qi_ref, slice(r * ts, (r + 1) * ts), IDX_HEADS) for r in range(tq // ts)]

    def score_chunk(c, carry):
        kchunk_t = kibt_ref[c]
        for r in range(tq // ts):
            rows = slice(r * ts, (r + 1) * ts)
            w = misc_ref[rows, MISC_WI:MISC_WI + IDX_HEADS] * (IDX_HEADS ** -0.5)
            d = jnp.dot(qi_stack[r], kchunk_t, preferred_element_type=F32)
            acc = jnp.zeros((ts, kc), F32)
            for h in range(IDX_HEADS):
                acc = acc + jnp.maximum(d[h * ts:(h + 1) * ts], 0.0) * w[:, h:h + 1]
            tpos = i * tq + r * ts + lax.broadcasted_iota(I32, (ts, kc), 0)
            key = jnp.where(c * kc + lax.broadcasted_iota(I32, (ts, kc), 1) <= tpos, _score_keys(acc), INT_MIN)
            keys_ref[c, rows, :] = key
            key_t = key.T
            hi_ref[c, :, rows] = (key_t >> 16).astype(I16)
            lo_ref[c, :, rows] = ((key_t & 0xFFFF) + I16_MIN).astype(I16)
        return carry
    lax.fori_loop(0, nkc, score_chunk, 0)

    def count_where(indicator):
        def body(c, cnt):
            return cnt + _lane_fold(indicator(c, keys_ref[c]))
        cnt = lax.fori_loop(0, nkc, body, jnp.zeros((tq, LANE), F32))
        return jnp.sum(cnt, axis=1, keepdims=True)

    pack = 2 * SUBLANE

    def tiles16(ref, c):
        return ref[c].reshape(kc // pack, pack, tq)

    def rows16(x):
        return jnp.concatenate([x, x], axis=0).astype(I16)

    def count16(ref, indicator):
        def body(c, cnt):
            ind = indicator(tiles16(ref, c))
            return cnt + functools.reduce(jnp.add, [ind[j] for j in range(kc // pack)])
        cnt = lax.fori_loop(0, nkc, body, jnp.zeros((pack, tq), I16)).astype(F32)
        cnt = cnt[:SUBLANE] + cnt[SUBLANE:]
        for shift in (4, 2, 1):
            cnt = cnt + pltpu.roll(cnt, shift, axis=0)
        return cnt

    one16, zero16 = jnp.int16(1), jnp.int16(0)
    thr_hi = _kth_largest_half(
        lambda cand: count16(hi_ref, lambda v: jnp.where(v >= rows16(cand)[None], one16, zero16)),
        (SUBLANE, tq), float(topk))
    thr_hi16 = rows16(thr_hi)[None]
    need_lo = float(topk) - count16(hi_ref, lambda v: jnp.where(v > thr_hi16, one16, zero16))

    def keep_bucket(c, carry):
        kept = jnp.where(tiles16(hi_ref, c) == thr_hi16, tiles16(lo_ref, c), jnp.int16(I16_MIN))
        lo_ref[c] = kept.reshape(kc, tq)
        return carry
    lax.fori_loop(0, nkc, keep_bucket, 0)
    thr_lo = _kth_largest_half(
        lambda cand: count16(lo_ref, lambda v: jnp.where(v >= rows16(cand)[None], one16, zero16)),
        (SUBLANE, tq), need_lo)
    thr_t = jnp.left_shift(thr_hi, 16) | (thr_lo - I16_MIN)
    thr_rows = jnp.transpose(jnp.broadcast_to(thr_t[0:1], (LANE, tq)))
    thr = thr_rows[:, 0:1]
    n_ge = count_where(lambda c, k: jnp.where(k >= thr, 1.0, 0.0))
    surplus = jnp.where(n_ge > topk, jnp.where(thr != INT_MIN, 1, 0), 0)

    @pl.when(jnp.max(surplus) > 0)
    def _():
        need = topk - count_where(lambda c, k: jnp.where(k > thr, 1.0, 0.0))
        nbits = max(1, (seq - 1).bit_length())
        cut = _tie_cut(lambda cand: count_where(
            lambda c, k: jnp.where(k == thr, jnp.where(c * kc + lane_pos < cand, 1.0, 0.0), 0.0)), need, tq, nbits)

        def drop(c, carry):
            k = keys_ref[c]
            keys_ref[c] = k - jnp.where(k == thr, jnp.where(c * kc + lane_pos > cut, surplus, 0), 0)
            return carry
        lax.fori_loop(0, nkc, drop, 0)

    thr = jnp.maximum(thr, INT_MIN + 1)

    v_lanes = lax.broadcasted_iota(I32, (kc, LANE), 1) // HEAD_DIM
    thr_ref[...] = jnp.broadcast_to(thr, (tq, LANE))
    for r in range(tq // ts):
        rows = slice(r * ts, (r + 1) * ts)
        nkc_r = (i * tq + (r + 1) * ts + kc - 1) // kc
        thr_r = thr_ref[rows, 0:1]
        qa_stack = head_stack(qa_ref, rows, N_HEADS)

        def score_pass(c, mrun):
            sel = keys_ref[c, rows, :] >= thr_r
            s = jnp.dot(qa_stack, kabt_ref[c], preferred_element_type=F32)
            s = jnp.where(sel[None], s.reshape(N_HEADS, ts, kc), -jnp.inf).reshape(N_HEADS * ts, kc)
            out = []
            for g in range(N_KV_HEADS):
                s_g = s[g * GQA_GROUP * ts:(g + 1) * GQA_GROUP * ts]
                s_ref[c, g] = s_g
                out.append(jnp.maximum(mrun[g], _lane_fold(s_g, jnp.maximum)))
            return tuple(out)
        mrun = lax.fori_loop(0, nkc_r, score_pass,
                             tuple(jnp.full((GQA_GROUP * ts, LANE), -jnp.inf, F32) for _ in range(N_KV_HEADS)))
        m = [jnp.max(mrun[g], axis=1, keepdims=True) for g in range(N_KV_HEADS)]

        def value_pass(c, acc):
            vchunk = vab_ref[pl.ds(pl.multiple_of(c * kc, kc), kc), :]
            out = []
            for g in range(N_KV_HEADS):
                vaug = jnp.where(v_lanes == g, vchunk, jnp.ones_like(vchunk))
                p = jnp.exp(s_ref[c, g] - m[g]).astype(BF)
                out.append(acc[g] + jnp.dot(p, vaug, preferred_element_type=F32))
            return tuple(out)
        acc = lax.fori_loop(0, nkc_r, value_pass,
                            tuple(jnp.zeros((GQA_GROUP * ts, LANE), F32) for _ in range(N_KV_HEADS)))

        for g in range(N_KV_HEADS):
            ones_lane = (1 - g) * HEAD_DIM
            o = acc[g] / acc[g][:, ones_lane:ones_lane + 1]
            for j in range(GQA_GROUP):
                h = g * GQA_GROUP + j
                out_ref[rows, h * HEAD_DIM:(h + 1) * HEAD_DIM] = (
                    o[j * ts:(j + 1) * ts, g * HEAD_DIM:(g + 1) * HEAD_DIM].astype(out_ref.dtype))


def _dsa_prompt(qi, misc, miscb_t, qa, kab_t, vab, batch, seq, tq, ts, kc):
    topk = min(TOPK_MAX, seq // 4)
    nq = seq // tq
    kern = functools.partial(_dsa_prompt_kernel, tq=tq, ts=ts, kc=kc, topk=topk, seq=seq)
    qrow = lambda b, i: (b * nq + i, 0)
    brow = lambda b, i: (b, 0)
    bchunks = pl.BlockSpec((seq // kc, LANE, kc), lambda b, i: (b, 0, 0))
    return pl.pallas_call(
        kern,
        grid=(batch, nq),
        in_specs=[pl.BlockSpec((tq, IDX_HEADS * LANE), qrow), pl.BlockSpec((tq, LANE), qrow),
                  bchunks, pl.BlockSpec((tq, N_HEADS * LANE), qrow),
                  bchunks, pl.BlockSpec((seq, LANE), brow)],
        out_specs=pl.BlockSpec((tq, N_HEADS * HEAD_DIM), qrow),
        out_shape=jax.ShapeDtypeStruct((batch * seq, N_HEADS * HEAD_DIM), BF),
        scratch_shapes=[pltpu.VMEM((seq // kc, tq, kc), I32),
                        pltpu.VMEM((seq // kc, kc, tq), I16),
                        pltpu.VMEM((seq // kc, kc, tq), I16),
                        pltpu.VMEM((seq // kc, N_KV_HEADS, GQA_GROUP * ts, kc), F32),
                        pltpu.VMEM((tq, LANE), I32)],
        compiler_params=pltpu.CompilerParams(dimension_semantics=("parallel", "arbitrary"),
                                             vmem_limit_bytes=VMEM_LIMIT),
        name="dsa_prompt",
    )(qi, misc, miscb_t, qa, kab_t, vab)


def _log_sigmoid(x):
    return jnp.minimum(x, 0.0) - jnp.log1p(jnp.exp(-jnp.abs(x)))


def _pad_rows(x, n):
    if x.shape[0] == n:
        return x
    return jnp.concatenate([x, jnp.zeros((n - x.shape[0], x.shape[1]), x.dtype)], axis=0)


def _gla_kernel(*refs, chunk, nchunks, has_s0):
    if has_s0:
        (misc_ref, wal_ref, bal_ref, q_ref, k_ref, v_ref, g_ref, gn_ref, s0_ref, o_ref, sout_ref, s_ref) = refs
    else:
        (misc_ref, wal_ref, bal_ref, q_ref, k_ref, v_ref, g_ref, gn_ref, o_ref, sout_ref, s_ref) = refs
        s0_ref = None
    t = pl.program_id(2)
    rows = chunk * nchunks
    cpad = max(chunk, 2 * SUBLANE)
    rpad = max(rows, 2 * SUBLANE)

    @pl.when(t == 0)
    def _():
        s_ref[...] = jnp.zeros(s_ref.shape, F32) if s0_ref is None else s0_ref[0, 0]

    la = jnp.dot(_pad_rows(misc_ref[0], rpad).astype(BF), wal_ref[0], preferred_element_type=F32)[:rows]
    la = _log_sigmoid(la + bal_ref[0]) / GATE_TAU
    pos = lax.broadcasted_iota(I32, (rows, GLA_DK), 0) % chunk
    cum = la
    d = 1
    while d < chunk:
        cum = cum + jnp.where(pos >= d, pltpu.roll(cum, d, axis=0), 0.0)
        d *= 2

    causal = (lax.broadcasted_iota(I32, (cpad, cpad), 0) >= lax.broadcasted_iota(I32, (cpad, cpad), 1))
    gn = gn_ref[...]
    for c in range(nchunks):
        r = slice(c * chunk, (c + 1) * chunk)
        cum_c = cum[r]
        last = cum_c[chunk - 1:chunk]
        q_c = q_ref[0, r, :] * (GLA_DK ** -0.5)
        k_c = k_ref[0, r, :]
        v_b = _pad_rows(v_ref[0, r, :], cpad).astype(BF)
        qt = _pad_rows(q_c * jnp.exp(cum_c), cpad).astype(BF)
        kt = _pad_rows(k_c * jnp.exp(-cum_c), cpad).astype(BF)
        kd = _pad_rows(k_c * jnp.exp(last - cum_c), cpad).astype(BF)
        att = lax.dot_general(qt, kt, (((1,), (1,)), ((), ())), preferred_element_type=F32)
        att = jnp.where(causal, att, 0.0).astype(BF)
        s_old = s_ref[...]
        o = (jnp.dot(att, v_b, preferred_element_type=F32)
             + jnp.dot(qt, s_old.astype(BF), preferred_element_type=F32))[:chunk]
        decay = jnp.transpose(jnp.broadcast_to(jnp.exp(last), (GLA_DK, GLA_DK)))
        upd = lax.dot_general(kd, v_b, (((0,), (0,)), ((), ())), preferred_element_type=F32)
        s_ref[...] = jnp.concatenate([decay] * (GLA_DV // GLA_DK), axis=1) * s_old + upd
        o = o * lax.rsqrt(jnp.mean(o * o, axis=-1, keepdims=True) + NORM_EPS) * gn
        gate = g_ref[0, r, :]
        o_ref[0, r, :] = (o * (gate * jax.nn.sigmoid(gate))).astype(o_ref.dtype)

    @pl.when(t == pl.num_programs(2) - 1)
    def _():
        sout_ref[0, 0] = s_ref[...]


def _gla(misc, wal, bal, qb, kb, vb, gb, gn, s0, chunk, nchunks, out_dtype):
    batch, seq, _ = misc.shape
    rows = chunk * nchunks
    head = lambda b, h, t: (b, t, h)
    in_specs = [pl.BlockSpec((1, rows, LANE), lambda b, h, t: (b, t, 0)),
                pl.BlockSpec((1, LANE, GLA_DK), lambda b, h, t: (h, 0, 0)),
                pl.BlockSpec((1, 1, GLA_DK), lambda b, h, t: (h, 0, 0)),
                pl.BlockSpec((1, rows, GLA_DK), head), pl.BlockSpec((1, rows, GLA_DK), head),
                pl.BlockSpec((1, rows, GLA_DV), head), pl.BlockSpec((1, rows, GLA_DV), head),
                pl.BlockSpec((1, GLA_DV), lambda b, h, t: (0, 0))]
    args = [misc, wal, bal, qb, kb, vb, gb, gn]
    state_spec = pl.BlockSpec((1, 1, GLA_DK, GLA_DV), lambda b, h, t: (b, h, 0, 0))
    if s0 is not None:
        in_specs.append(state_spec)
        args.append(s0)
    kern = functools.partial(_gla_kernel, chunk=chunk, nchunks=nchunks, has_s0=s0 is not None)
    return pl.pallas_call(
        kern,
        grid=(batch, GLA_HEADS, seq // rows),
        in_specs=in_specs,
        out_specs=[pl.BlockSpec((1, rows, GLA_DV), head), state_spec],
        out_shape=[jax.ShapeDtypeStruct((batch, seq, GLA_DV_TOT), out_dtype),
                   jax.ShapeDtypeStruct((batch, GLA_HEADS, GLA_DK, GLA_DV), F32)],
        scratch_shapes=[pltpu.VMEM((GLA_DK, GLA_DV), F32)],
        compiler_params=pltpu.CompilerParams(dimension_semantics=("parallel", "parallel", "arbitrary"),
                                             vmem_limit_bytes=VMEM_LIMIT),
        name="gla",
    )(*args)


def _layer_norm(x, g, b):
    xc = x - jnp.mean(x, axis=-1, keepdims=True)
    var = jnp.mean(xc * xc, axis=-1, keepdims=True)
    return xc * lax.rsqrt(var + NORM_EPS) * g + b


def _post_kernel(x_ref, attn_ref, ob_ref, ga_ref, gb_ref, wao_ref, wgo_ref, wout_ref, ln1g_ref, ln1b_ref,
                 wf1_ref, wf2_ref, ln2g_ref, ln2b_ref, out_ref, *, ff_chunk):
    branch_a = jnp.dot(attn_ref[...], wao_ref[...], preferred_element_type=F32)
    branch_b = jnp.dot(ob_ref[...], wgo_ref[...], preferred_element_type=F32)
    merged = jax.nn.sigmoid(ga_ref[...]) * branch_a + jax.nn.sigmoid(gb_ref[...]) * branch_b
    mix = jnp.dot(merged.astype(BF), wout_ref[...], preferred_element_type=F32)
    h = _layer_norm(DN_ALPHA * x_ref[...] + mix, ln1g_ref[...], ln1b_ref[...])
    hb = h.astype(BF)
    ff = jnp.zeros(h.shape, F32)
    for c in range(D_FF // ff_chunk):
        cols = slice(c * ff_chunk, (c + 1) * ff_chunk)
        a = jnp.maximum(jnp.dot(hb, wf1_ref[:, cols], preferred_element_type=F32), 0.0)
        ff = ff + jnp.dot((a * a).astype(BF), wf2_ref[cols, :], preferred_element_type=F32)
    out_ref[...] = _layer_norm(DN_ALPHA * h + ff, ln2g_ref[...], ln2b_ref[...])


def _post(x2d, attn, ob, ga, gb, wao, wgo, wout, ln1g, ln1b, wf1, wf2, ln2g, ln2b, tm):
    m = x2d.shape[0]
    row = lambda width: pl.BlockSpec((tm, width), lambda i: (i, 0))
    return pl.pallas_call(
        functools.partial(_post_kernel, ff_chunk=1024),
        grid=(m // tm,),
        in_specs=[row(D_MODEL), row(N_HEADS * HEAD_DIM), row(GLA_DV_TOT), row(D_MODEL), row(D_MODEL),
                  _const_spec(wao.shape), _const_spec(wgo.shape), _const_spec(wout.shape),
                  _const_spec(ln1g.shape), _const_spec(ln1b.shape), _const_spec(wf1.shape), _const_spec(wf2.shape),
                  _const_spec(ln2g.shape), _const_spec(ln2b.shape)],
        out_specs=row(D_MODEL),
        out_shape=jax.ShapeDtypeStruct((m, D_MODEL), F32),
        compiler_params=pltpu.CompilerParams(dimension_semantics=("parallel",), vmem_limit_bytes=VMEM_LIMIT),
        name="post",
    )(x2d, attn, ob, ga, gb, wao, wgo, wout, ln1g, ln1b, wf1, wf2, ln2g, ln2b)


def _dsa_sample_kernel(pt_ref, qi_ref, misc_ref, qa_ref, ka_ref, va_ref,
                       ckidx_ref, ck_ref, cv_ref, out_ref,
                       ibuf, kbuf, vbuf, isem, ksem, vsem, keys_ref, *, t, n_pages, ppc, topk):
    b = pl.program_id(0)
    nch = n_pages // ppc
    kc = ppc * PAGE_SIZE
    past = n_pages * PAGE_SIZE
    hq = IDX_HEADS * t

    def page_copy(src_ref, buf, sem, ch, slot, p):
        return pltpu.make_async_copy(src_ref.at[pt_ref[b, ch * ppc + p]],
                                     buf.at[slot, ..., pl.ds(p * PAGE_SIZE, PAGE_SIZE)], sem.at[slot])

    def start_chunk(src_ref, buf, sem, ch, slot):
        for p in range(ppc):
            page_copy(src_ref, buf, sem, ch, slot, p).start()

    def wait_chunk(src_ref, buf, sem, ch, slot):
        for p in range(ppc):
            page_copy(src_ref, buf, sem, ch, slot, p).wait()

    qi_all = jnp.concatenate([qi_ref[0, :, h * LANE:h * LANE + IDX_DIM] for h in range(IDX_HEADS)],
                             axis=0).astype(BF)
    misc = misc_ref[0]
    w = misc[:, MISC_WI:MISC_WI + IDX_HEADS] * (IDX_HEADS ** -0.5)
    w_all = jnp.concatenate([w[:, h:h + 1] for h in range(IDX_HEADS)], axis=0)
    qpos = past + lax.broadcasted_iota(I32, (t, LANE), 0)

    nt_dims = (((1,), (1,)), ((), ()))

    def index_scores(d):
        r = jnp.maximum(d, 0.0) * w_all
        return jnp.sum(r.reshape(IDX_HEADS, t, r.shape[1]), axis=0)

    start_chunk(ckidx_ref, ibuf, isem, 0, 0)
    for ch in range(nch):
        slot = ch % 2
        if ch + 1 < nch:
            start_chunk(ckidx_ref, ibuf, isem, ch + 1, 1 - slot)
        wait_chunk(ckidx_ref, ibuf, isem, ch, slot)
        d = jnp.dot(qi_all, ibuf[slot].astype(BF), preferred_element_type=F32)
        keys_ref[:, ch * kc:(ch + 1) * kc] = _score_keys(index_scores(d))
    knew = _pad_rows(misc[:, :IDX_DIM], LANE).astype(BF)
    d_new = lax.dot_general(qi_all, knew, nt_dims, preferred_element_type=F32)
    new_pos = past + lax.broadcasted_iota(I32, (t, LANE), 1)
    keys_ref[:, past:past + LANE] = jnp.where(new_pos <= qpos, _score_keys(index_scores(d_new)), INT_MIN)

    width = past + LANE
    col = lax.broadcasted_iota(I32, (t, width), 1)

    def count_where(indicator):
        return jnp.sum(_lane_fold(indicator(keys_ref[...])), axis=1, keepdims=True)

    thr = _kth_largest_key(lambda cand: count_where(lambda k: jnp.where(k >= cand, 1, 0)), t, topk)
    n_ge = count_where(lambda k: jnp.where(k >= thr, 1, 0))
    surplus = jnp.where(n_ge > topk, jnp.where(thr != INT_MIN, 1, 0), 0)

    @pl.when(jnp.max(surplus) > 0)
    def _():
        need = topk - count_where(lambda k: jnp.where(k > thr, 1, 0))
        nbits = max(1, (width - 1).bit_length())
        cut = _tie_cut(lambda cand: count_where(
            lambda k: jnp.where(k == thr, jnp.where(col < cand, 1, 0), 0)), need, t, nbits)
        k = keys_ref[...]
        keys_ref[...] = k - jnp.where(k == thr, jnp.where(col > cut, surplus, 0), 0)

    thr = jnp.maximum(thr, INT_MIN + 1)

    gt = GQA_GROUP * t
    qg = [jnp.concatenate([qa_ref[0, :, h * LANE + g * HEAD_DIM:h * LANE + (g + 1) * HEAD_DIM]
                           for h in range(g * GQA_GROUP, (g + 1) * GQA_GROUP)], axis=0).astype(BF)
          for g in range(N_KV_HEADS)]

    def attend(carry, scores, values, sel):
        m_old, l_old, acc_old = carry
        s = jnp.concatenate([scores(g) for g in range(N_KV_HEADS)], axis=0)
        n = s.shape[1]
        s = jnp.where(sel[None], s.reshape(N_HEADS, t, n), -jnp.inf).reshape(N_HEADS * t, n)
        m_new = jnp.maximum(m_old, jnp.max(s, axis=1, keepdims=True))
        alpha = jnp.exp(m_old - m_new)
        p = jnp.exp(s - m_new)
        l_new = alpha * l_old + jnp.sum(p, axis=1, keepdims=True)
        pb = p.astype(BF)
        pv = jnp.concatenate([values(g, pb[g * gt:(g + 1) * gt]) for g in range(N_KV_HEADS)], axis=0)
        return m_new, l_new, alpha * acc_old + pv

    carry = (jnp.full((hq, 1), NEG_BIG, F32), jnp.zeros((hq, 1), F32), jnp.zeros((hq, HEAD_DIM), F32))
    start_chunk(ck_ref, kbuf, ksem, 0, 0)
    start_chunk(cv_ref, vbuf, vsem, 0, 0)
    for ch in range(nch):
        slot = ch % 2
        if ch + 1 < nch:
            start_chunk(ck_ref, kbuf, ksem, ch + 1, 1 - slot)
            start_chunk(cv_ref, vbuf, vsem, ch + 1, 1 - slot)
        wait_chunk(ck_ref, kbuf, ksem, ch, slot)
        wait_chunk(cv_ref, vbuf, vsem, ch, slot)
        sel = keys_ref[:, ch * kc:(ch + 1) * kc] >= thr
        carry = attend(
            carry,
            lambda g: jnp.dot(qg[g], kbuf[slot, g].astype(BF), preferred_element_type=F32),
            lambda g, pb: lax.dot_general(pb, vbuf[slot, g].astype(BF), nt_dims, preferred_element_type=F32),
            sel)
    sel_new = keys_ref[:, past:past + LANE] >= thr
    k_new = _pad_rows(ka_ref[0], LANE)
    v_new = _pad_rows(va_ref[0], LANE)
    carry = attend(
        carry,
        lambda g: lax.dot_general(qg[g], k_new[:, g * HEAD_DIM:(g + 1) * HEAD_DIM].astype(BF), nt_dims,
                                  preferred_element_type=F32),
        lambda g, pb: jnp.dot(pb, v_new[:, g * HEAD_DIM:(g + 1) * HEAD_DIM].astype(BF), preferred_element_type=F32),
        sel_new)
    _, l_fin, acc_fin = carry
    o = acc_fin / l_fin
    for h in range(N_HEADS):
        out_ref[0, :, h * HEAD_DIM:(h + 1) * HEAD_DIM] = o[h * t:(h + 1) * t].astype(out_ref.dtype)


def _dsa_sample(page_table, qi, misc, qa, ka, va, cache_kidx, cache_k, cache_v, ppc):
    batch, t, _ = misc.shape
    n_pages = page_table.shape[1]
    past = n_pages * PAGE_SIZE
    topk = min(TOPK_MAX, (past + t) // 4)
    kern = functools.partial(_dsa_sample_kernel, t=t, n_pages=n_pages, ppc=ppc, topk=topk)
    row = lambda width: pl.BlockSpec((1, t, width), lambda b, pt: (b, 0, 0))
    hbm = pl.BlockSpec(memory_space=pl.ANY)
    grid_spec = pltpu.PrefetchScalarGridSpec(
        num_scalar_prefetch=1,
        grid=(batch,),
        in_specs=[row(IDX_HEADS * LANE), row(LANE), row(N_HEADS * LANE), row(LANE), row(LANE),
                  hbm, hbm, hbm],
        out_specs=row(N_HEADS * HEAD_DIM),
        scratch_shapes=[pltpu.VMEM((2, IDX_DIM, ppc * PAGE_SIZE), F32),
                        pltpu.VMEM((2, N_KV_HEADS, HEAD_DIM, ppc * PAGE_SIZE), F32),
                        pltpu.VMEM((2, N_KV_HEADS, HEAD_DIM, ppc * PAGE_SIZE), F32),
                        pltpu.SemaphoreType.DMA((2,)), pltpu.SemaphoreType.DMA((2,)), pltpu.SemaphoreType.DMA((2,)),
                        pltpu.VMEM((t, past + LANE), I32)],
    )
    return pl.pallas_call(
        kern,
        grid_spec=grid_spec,
        out_shape=jax.ShapeDtypeStruct((batch, t, N_HEADS * HEAD_DIM), F32),
        compiler_params=pltpu.CompilerParams(dimension_semantics=("arbitrary",), vmem_limit_bytes=VMEM_LIMIT),
        name="dsa_sample",
    )(page_table, qi, misc, qa, ka, va, cache_kidx, cache_k, cache_v)


def _row_tile(m, cap):
    tm = min(m, cap)
    while m % tm:
        tm //= 2
    return tm


def _gla_and_post(x2d, attn, misc, gla_in, gates, weights, batch, seq, gla_s0, gla_chunk, gla_nchunks, gla_out_dtype, tm):
    (_, wal, bal, gn, wao, wgo, wout, ln1g, ln1b, wf1, wf2, ln2g, ln2b) = weights
    b3 = lambda a: a.reshape(batch, seq, a.shape[-1])
    qb, kb, vb, gb = gla_in
    ob, s_new = _gla(b3(misc), wal, bal, b3(qb), b3(kb), b3(vb), b3(gb), gn, gla_s0,
                     gla_chunk, gla_nchunks, gla_out_dtype)
    ob = ob.reshape(batch * seq, GLA_DV_TOT).astype(BF)
    y = _post(x2d, attn, ob, *gates, wao, wgo, wout, ln1g, ln1b, wf1, wf2, ln2g, ln2b, tm)
    return y, s_new


def kernel(x_prompt, x_sample, cache_k, cache_v, cache_kidx, state_gla, page_table,
           w_in, w_alpha2, b_alpha, gla_norm_g, w_attn_o, w_gla_o, w_out,
           ln1_g, ln1_b, w_ff1, w_ff2, ln2_g, ln2_b):
    assert w_in.shape[0] == DEPTH == 1
    bp, tp, _ = x_prompt.shape
    bs, ts, _ = x_sample.shape
    n_pool = cache_k.shape[1]
    l = 0
    wal = jnp.zeros((LANE, GLA_DK_TOT), F32).at[MISC_AB:MISC_AB + GATE_RANK].set(w_alpha2[l])
    wal = wal.reshape(LANE, GLA_HEADS, GLA_DK).transpose(1, 0, 2).astype(BF)
    weights = (_pack_w_in(w_in[l]), wal, b_alpha[l].reshape(GLA_HEADS, 1, GLA_DK), gla_norm_g[l].reshape(1, GLA_DV),
               w_attn_o[l].astype(BF), w_gla_o[l].astype(BF), w_out[l].astype(BF),
               ln1_g[l].reshape(1, D_MODEL), ln1_b[l].reshape(1, D_MODEL),
               w_ff1[l].astype(BF), w_ff2[l].astype(BF), ln2_g[l].reshape(1, D_MODEL), ln2_b[l].reshape(1, D_MODEL))

    xp = x_prompt.reshape(bp * tp, D_MODEL)
    kc = min(512, tp)
    tq = min(256, tp)
    (qa, ka_t, va_t, kab_t, vab, qi, misc, ki_t, miscb_t, *rest) = _proj(xp, weights[0], kc, token_minor_seq=tp)
    attn = _dsa_prompt(qi, misc, miscb_t, qa, kab_t, vab, batch=bp, seq=tp, tq=tq, ts=min(128, tq), kc=kc)
    gchunk = min(GLA_CHUNK, tp)
    y_p, s_p = _gla_and_post(xp, attn, misc, rest[:4], rest[4:], weights, bp, tp, None, gchunk,
                             min(8, tp // gchunk), BF, kc)

    def kv_t(a):
        return jnp.transpose(a.reshape(bp, N_KV_HEADS, HEAD_DIM, tp), (0, 3, 1, 2))[None]

    xs = x_sample.reshape(bs * ts, D_MODEL)
    tm_s = _row_tile(bs * ts, 512)
    (qa, ka, va, qi, misc, *rest) = _proj(xs, weights[0], tm_s)
    ckidx = jnp.transpose(cache_kidx[l], (0, 2, 1))
    ck = jnp.transpose(cache_k[l], (0, 2, 3, 1))
    cv = jnp.transpose(cache_v[l], (0, 2, 3, 1))
    f3 = lambda a: a.astype(F32).reshape(bs, ts, a.shape[-1])
    attn = _dsa_sample(page_table, f3(qi), f3(misc), f3(qa), f3(ka), f3(va), ckidx, ck, cv,
                       ppc=min(16, page_table.shape[1]))
    attn = attn.reshape(bs * ts, N_HEADS * HEAD_DIM).astype(BF)
    y_s, s_s = _gla_and_post(xs, attn, misc, rest[:4], rest[4:], weights, bs, ts, state_gla[l], ts, 1, F32, tm_s)

    def kv(a):
        return a.reshape(1, bs, ts, N_KV_HEADS, HEAD_DIM)

    return (y_p.reshape(bp, tp, D_MODEL), y_s.reshape(bs, ts, D_MODEL),
            kv_t(ka_t), kv_t(va_t), jnp.transpose(ki_t, (0, 2, 1))[None], s_p[None],
            kv(ka), kv(va), misc[:, :IDX_DIM].reshape(1, bs, ts, IDX_DIM), s_s[None])
```

```python
import functools

import jax
import jax.numpy as jnp
from jax import lax
from jax.experimental import pallas as pl
from jax.experimental.pallas import tpu as pltpu

D_MODEL = 1024
N_HEADS = 8
N_KV_HEADS = 2
HEAD_DIM = 64
GQA_GROUP = N_HEADS // N_KV_HEADS
IDX_HEADS = 8
IDX_DIM = 64
TOPK_MAX = 256
PAGE_SIZE = 128
GLA_HEADS = 4
GLA_DK_TOT = D_MODEL // 2
GLA_DV_TOT = D_MODEL
GLA_DK = GLA_DK_TOT // GLA_HEADS
GLA_DV = GLA_DV_TOT // GLA_HEADS
GATE_RANK = 16
GATE_TAU = 16.0
GLA_CHUNK = 64
D_FF = 4 * D_MODEL
NORM_EPS = 1e-5
DEPTH = 1
DN_ALPHA = (2 * DEPTH) ** 0.25

LANE = 128
SUBLANE = 8
VMEM_LIMIT = 56 * 1024 * 1024
BF = jnp.bfloat16
F32 = jnp.float32
I32 = jnp.int32
I16 = jnp.int16
INT_MIN = -2 ** 31
I16_MIN = -2 ** 15
LOG2_E = 1.4426950408889634
KEY_LOWEST_FINITE = -0x7F7FFFFF
NEG_BIG = -1e30

MISC_WI = IDX_DIM
MISC_AB = IDX_DIM + IDX_HEADS

_PROJ_GROUPS = (
    ('qa', N_HEADS * LANE), ('ka', LANE), ('va', LANE), ('qi', IDX_HEADS * LANE), ('misc', LANE),
    ('qb', GLA_DK_TOT), ('kb', GLA_DK_TOT), ('vb', GLA_DV_TOT), ('gb', GLA_DV_TOT),
    ('ga', D_MODEL), ('gb2', D_MODEL),
)
_PROJ_OFF = {}
_o = 0
for _n, _w in _PROJ_GROUPS:
    _PROJ_OFF[_n] = (_o, _o + _w)
    _o += _w
PROJ_COLS = _o


def _const_spec(shape):
    nd = len(shape)
    return pl.BlockSpec(shape, lambda *_: (0,) * nd, pipeline_mode=pl.Buffered(1))


def _pack_w_in(w_in):
    off = 0
    parts = {}
    for name, width in (('q_a', 512), ('k_a', 128), ('v_a', 128), ('q_i', 512), ('k_i', 64), ('w_i', 8),
                        ('q_b', 512), ('k_b', 512), ('v_b', 1024), ('g_b', 1024), ('a_b', 16),
                        ('gate_a', 1024), ('gate_b', 1024)):
        parts[name] = w_in[:, off:off + width]
        off += width
    z64 = jnp.zeros((D_MODEL, HEAD_DIM), F32)
    qa_cols = []
    for h in range(N_HEADS):
        q = parts['q_a'][:, h * HEAD_DIM:(h + 1) * HEAD_DIM] * (HEAD_DIM ** -0.5 * LOG2_E)
        qa_cols += [q, z64] if h // GQA_GROUP == 0 else [z64, q]
    qi_cols = []
    for h in range(IDX_HEADS):
        qi_cols += [parts['q_i'][:, h * IDX_DIM:(h + 1) * IDX_DIM] * (IDX_DIM ** -0.5), z64]
    misc = jnp.concatenate([parts['k_i'], parts['w_i'], parts['a_b'],
                            jnp.zeros((D_MODEL, LANE - MISC_AB - GATE_RANK), F32)], axis=1)
    cols = qa_cols + [parts['k_a'], parts['v_a']] + qi_cols + [
        misc, parts['q_b'], parts['k_b'], parts['v_b'], parts['g_b'], parts['gate_a'], parts['gate_b']]
    return jnp.concatenate(cols, axis=1).astype(BF)


def _proj_kernel(x_ref, w_ref, *out_refs, token_minor):
    xb = x_ref[...].astype(BF)

    def mm(name):
        a, b = _PROJ_OFF[name]
        return jnp.dot(xb, w_ref[:, a:b], preferred_element_type=F32)

    if token_minor:
        (qa_ref, kat_ref, vat_ref, kabt_ref, vab_ref, qi_ref, misc_ref, kit_ref, miscbt_ref,
         qb_ref, kb_ref, vb_ref, gb_ref, ga_ref, gb2_ref, ka_tmp, va_tmp) = out_refs
    else:
        (qa_ref, ka_ref, va_ref, qi_ref, misc_ref, qb_ref, kb_ref, vb_ref, gb_ref, ga_ref, gb2_ref) = out_refs
    qa_ref[...] = mm('qa').astype(BF)
    qi_ref[...] = mm('qi').astype(BF)
    ka, va, misc = mm('ka'), mm('va'), mm('misc')
    misc_ref[...] = misc
    if token_minor:
        ka_tmp[...] = ka
        va_tmp[...] = va
        ka_t, va_t, misc_t = ka_tmp[...].T, va_tmp[...].T, misc_ref[...].T
        kat_ref[0] = ka_t
        vat_ref[0] = va_t
        kit_ref[0] = misc_t[:IDX_DIM]
        kabt_ref[0] = ka_t.astype(BF)
        miscbt_ref[0] = misc_t.astype(BF)
        vab_ref[...] = va.astype(BF)
    else:
        ka_ref[...] = ka
        va_ref[...] = va
    qb_ref[...] = mm('qb')
    kb_ref[...] = mm('kb')
    vb_ref[...] = mm('vb')
    gb_ref[...] = mm('gb')
    ga_ref[...] = mm('ga')
    gb2_ref[...] = mm('gb2')


def _proj(x2d, w_packed, tm, token_minor_seq=None):
    m = x2d.shape[0]
    widths = dict(_PROJ_GROUPS)
    row = lambda n, dt: (pl.BlockSpec((tm, widths[n]), lambda i: (i, 0)), jax.ShapeDtypeStruct((m, widths[n]), dt))
    if token_minor_seq is None:
        outs = [row('qa', BF), row('ka', F32), row('va', F32), row('qi', BF), row('misc', F32)]
    else:
        nt = token_minor_seq // tm
        seq_t = lambda rows: (pl.BlockSpec((1, rows, tm), lambda i: (i // nt, 0, i % nt)),
                              jax.ShapeDtypeStruct((m // token_minor_seq, rows, token_minor_seq), F32))
        chunk_t = (pl.BlockSpec((1, LANE, tm), lambda i: (i, 0, 0)), jax.ShapeDtypeStruct((m // tm, LANE, tm), BF))
        outs = [row('qa', BF), seq_t(LANE), seq_t(LANE), chunk_t, row('va', BF), row('qi', BF), row('misc', F32),
                seq_t(IDX_DIM), chunk_t]
    outs += [row(n, F32) for n in ('qb', 'kb', 'vb', 'gb', 'ga', 'gb2')]
    return pl.pallas_call(
        functools.partial(_proj_kernel, token_minor=token_minor_seq is not None),
        grid=(m // tm,),
        in_specs=[pl.BlockSpec((tm, D_MODEL), lambda i: (i, 0)), _const_spec((D_MODEL, PROJ_COLS))],
        out_specs=[spec for spec, _ in outs],
        out_shape=[shape for _, shape in outs],
        scratch_shapes=[] if token_minor_seq is None else [pltpu.VMEM((tm, LANE), F32)] * 2,
        compiler_params=pltpu.CompilerParams(dimension_semantics=("parallel",), vmem_limit_bytes=VMEM_LIMIT),
        name="proj",
    )(x2d, w_packed)


def _key_to_f32(key):
    return pltpu.bitcast(jnp.where(key < 0, INT_MIN - key, key), F32)


def _tree_reduce(op, xs):
    xs = list(xs)
    while len(xs) > 1:
        xs = [op(xs[j], xs[j + 1]) if j + 1 < len(xs) else xs[j] for j in range(0, len(xs), 2)]
    return xs[0]


def _lane_fold(x, op=jnp.add):
    return _tree_reduce(op, [x[:, j * LANE:(j + 1) * LANE] for j in range(x.shape[1] // LANE)])


def _radix_max(count_ge, start, nbits, k):
    def body(it, ans):
        cand = ans + jnp.left_shift(jnp.int32(1), nbits - 1 - it)
        return jnp.where(count_ge(cand) >= k, cand, ans)
    return lax.fori_loop(0, nbits, body, start)


def _tie_cut(count_tie_below, need, rows, nbits):
    def body(it, m):
        cand = m | jnp.left_shift(jnp.int32(1), nbits - 1 - it)
        return jnp.where(count_tie_below(cand) < need, cand, m)
    return lax.fori_loop(0, nbits, body, jnp.zeros((rows, 1), I32))


def _dsa_prompt_kernel(qi_ref, misc_ref, kibt_ref, qa_ref, kabt_ref, vab_ref, out_ref,
                       sc_ref, hi_ref, lo_ref, s_ref, thr_ref, *, tq, ts, kc, topk, seq):
    i = pl.program_id(1)
    nkc = ((i + 1) * tq + kc - 1) // kc
    lane_pos = lax.broadcasted_iota(I32, (tq, kc), 1)

    def head_stack(ref, rows, nheads):
        return jnp.concatenate([ref[rows, h * LANE:(h + 1) * LANE] for h in range(nheads)], axis=0)

    qi_stack = [head_stack(qi_ref, slice(r * ts, (r + 1) * ts), IDX_HEADS) for r in range(tq // ts)]

    def score_chunk(c, carry):
        kchunk_t = kibt_ref[c]
        for r in range(tq // ts):
            rows = slice(r * ts, (r + 1) * ts)
            w = misc_ref[rows, MISC_WI:MISC_WI + IDX_HEADS] * (IDX_HEADS ** -0.5)
            d = jnp.dot(qi_stack[r], kchunk_t, preferred_element_type=F32)
            acc = jnp.zeros((ts, kc), F32)
            for h in range(IDX_HEADS):
                acc = acc + jnp.maximum(d[h * ts:(h + 1) * ts], 0.0) * w[:, h:h + 1]
            tpos = i * tq + r * ts + lax.broadcasted_iota(I32, (ts, kc), 0)
            score = jnp.where(c * kc + lax.broadcasted_iota(I32, (ts, kc), 1) <= tpos, acc, -jnp.inf)
            sc_ref[c, rows, :] = score
            score_t = score.T
            lo_ref[c, :, rows] = score_t
            hi_ref[c, :, rows] = score_t.astype(BF)
        return carry
    lax.fori_loop(0, nkc, score_chunk, 0)

    def count_where(indicator):
        def body(c, cnt):
            return cnt + _lane_fold(indicator(c, sc_ref[c]))
        cnt = lax.fori_loop(0, nkc, body, jnp.zeros((tq, LANE), F32))
        return jnp.sum(cnt, axis=1, keepdims=True)

    def count_keys(ref, pack, zero, indicator):
        def body(c, cnt):
            ind = indicator(ref[c].reshape(kc // pack, pack, tq))
            return cnt + _tree_reduce(jnp.add, [ind[j] for j in range(kc // pack)])
        cnt = lax.fori_loop(0, nkc, body, jnp.zeros((pack, tq), zero.dtype)).astype(F32)
        cnt = _tree_reduce(jnp.add, [cnt[j * SUBLANE:(j + 1) * SUBLANE] for j in range(pack // SUBLANE)])
        for shift in (4, 2, 1):
            cnt = cnt + pltpu.roll(cnt, shift, axis=0)
        return cnt

    def bf16_rows(key16):
        bits = jnp.where(key16 < 0, I16_MIN - key16, key16)
        val = pltpu.bitcast(jnp.left_shift(bits, 16), F32)
        return jnp.concatenate([val, val], axis=0).astype(BF)[None]

    one16, zero16 = jnp.int16(1), jnp.int16(0)
    thr_hi = _radix_max(
        lambda cand: count_keys(hi_ref, 2 * SUBLANE, zero16, lambda v: jnp.where(v >= bf16_rows(cand), one16, zero16)),
        jnp.full((SUBLANE, tq), I16_MIN, I32), 16, float(topk))
    thr_key = _radix_max(
        lambda cand: count_keys(lo_ref, SUBLANE, jnp.float32(0),
                                lambda v: jnp.where(v >= _key_to_f32(cand)[None], 1.0, 0.0)),
        jnp.left_shift(thr_hi - 1, 16), 17, float(topk))
    thr_t = _key_to_f32(jnp.maximum(thr_key, KEY_LOWEST_FINITE))
    thr_rows = jnp.transpose(jnp.broadcast_to(thr_t[0:1], (LANE, tq)))
    thr = thr_rows[:, 0:1]
    n_ge = count_where(lambda c, s: jnp.where(s >= thr, 1.0, 0.0))
    surplus = jnp.where(n_ge > topk, 1, 0)

    @pl.when(jnp.max(surplus) > 0)
    def _():
        need = topk - count_where(lambda c, s: jnp.where(s > thr, 1.0, 0.0))
        nbits = max(1, (seq - 1).bit_length())
        cut = _tie_cut(lambda cand: count_where(
            lambda c, s: jnp.where(s == thr, jnp.where(c * kc + lane_pos < cand, 1.0, 0.0), 0.0)), need, tq, nbits)

        def drop(c, carry):
            s = sc_ref[c]
            dropped = jnp.where(s == thr, jnp.where(c * kc + lane_pos > cut, surplus, 0), 0)
            sc_ref[c] = jnp.where(dropped > 0, -jnp.inf, s)
            return carry
        lax.fori_loop(0, nkc, drop, 0)

    v_lanes = lax.broadcasted_iota(I32, (kc, LANE), 1) // HEAD_DIM
    thr_ref[...] = jnp.broadcast_to(thr, (tq, LANE))
    for r in range(tq // ts):
        rows = slice(r * ts, (r + 1) * ts)
        nkc_r = (i * tq + (r + 1) * ts + kc - 1) // kc
        thr_r = thr_ref[rows, 0:1]
        qa_stack = head_stack(qa_ref, rows, N_HEADS)

        def score_pass(c, mrun):
            sel = sc_ref[c, rows, :] >= thr_r
            s = jnp.dot(qa_stack, kabt_ref[c], preferred_element_type=F32)
            s = jnp.where(sel[None], s.reshape(N_HEADS, ts, kc), -jnp.inf).reshape(N_HEADS * ts, kc)
            out = []
            for g in range(N_KV_HEADS):
                s_g = s[g * GQA_GROUP * ts:(g + 1) * GQA_GROUP * ts]
                s_ref[c, g] = s_g
                out.append(jnp.maximum(mrun[g], _lane_fold(s_g, jnp.maximum)))
            return tuple(out)
        mrun = lax.fori_loop(0, nkc_r, score_pass,
                             tuple(jnp.full((GQA_GROUP * ts, LANE), -jnp.inf, F32) for _ in range(N_KV_HEADS)))
        m = [jnp.max(mrun[g], axis=1, keepdims=True) for g in range(N_KV_HEADS)]

        def value_pass(c, acc):
            vchunk = vab_ref[pl.ds(pl.multiple_of(c * kc, kc), kc), :]
            out = []
            for g in range(N_KV_HEADS):
                vaug = jnp.where(v_lanes == g, vchunk, jnp.ones_like(vchunk))
                p = jnp.exp2(s_ref[c, g] - m[g]).astype(BF)
                out.append(acc[g] + jnp.dot(p, vaug, preferred_element_type=F32))
            return tuple(out)
        acc = lax.fori_loop(0, nkc_r, value_pass,
                            tuple(jnp.zeros((GQA_GROUP * ts, LANE), F32) for _ in range(N_KV_HEADS)))

        for g in range(N_KV_HEADS):
            ones_lane = (1 - g) * HEAD_DIM
            o = acc[g] / acc[g][:, ones_lane:ones_lane + 1]
            for j in range(GQA_GROUP):
                h = g * GQA_GROUP + j
                out_ref[rows, h * HEAD_DIM:(h + 1) * HEAD_DIM] = (
                    o[j * ts:(j + 1) * ts, g * HEAD_DIM:(g + 1) * HEAD_DIM].astype(out_ref.dtype))


def _dsa_prompt(qi, misc, miscb_t, qa, kab_t, vab, batch, seq, tq, ts, kc):
    topk = min(TOPK_MAX, seq // 4)
    nq = seq // tq
    kern = functools.partial(_dsa_prompt_kernel, tq=tq, ts=ts, kc=kc, topk=topk, seq=seq)
    qrow = lambda b, i: (b * nq + i, 0)
    brow = lambda b, i: (b, 0)
    bchunks = pl.BlockSpec((seq // kc, LANE, kc), lambda b, i: (b, 0, 0))
    return pl.pallas_call(
        kern,
        grid=(batch, nq),
        in_specs=[pl.BlockSpec((tq, IDX_HEADS * LANE), qrow), pl.BlockSpec((tq, LANE), qrow),
                  bchunks, pl.BlockSpec((tq, N_HEADS * LANE), qrow),
                  bchunks, pl.BlockSpec((seq, LANE), brow)],
        out_specs=pl.BlockSpec((tq, N_HEADS * HEAD_DIM), qrow),
        out_shape=jax.ShapeDtypeStruct((batch * seq, N_HEADS * HEAD_DIM), BF),
        scratch_shapes=[pltpu.VMEM((seq // kc, tq, kc), F32),
                        pltpu.VMEM((seq // kc, kc, tq), BF),
                        pltpu.VMEM((seq // kc, kc, tq), F32),
                        pltpu.VMEM((seq // kc, N_KV_HEADS, GQA_GROUP * ts, kc), F32),
                        pltpu.VMEM((tq, LANE), F32)],
        compiler_params=pltpu.CompilerParams(dimension_semantics=("parallel", "arbitrary"),
                                             vmem_limit_bytes=VMEM_LIMIT),
        name="dsa_prompt",
    )(qi, misc, miscb_t, qa, kab_t, vab)


def _log_sigmoid(x):
    return jnp.minimum(x, 0.0) - jnp.log1p(jnp.exp(-jnp.abs(x)))


def _pad_rows(x, n):
    if x.shape[0] == n:
        return x
    return jnp.concatenate([x, jnp.zeros((n - x.shape[0], x.shape[1]), x.dtype)], axis=0)


def _gla_kernel(*refs, chunk, nchunks, heads, has_s0):
    if has_s0:
        (misc_ref, wal_ref, bal_ref, q_ref, k_ref, v_ref, g_ref, gn_ref, s0_ref, o_ref, sout_ref, s_ref) = refs
    else:
        (misc_ref, wal_ref, bal_ref, q_ref, k_ref, v_ref, g_ref, gn_ref, o_ref, sout_ref, s_ref) = refs
        s0_ref = None
    t = pl.program_id(2)
    rows = chunk * nchunks
    cpad = max(chunk, 2 * SUBLANE)
    rpad = max(rows, 2 * SUBLANE)

    @pl.when(t == 0)
    def _():
        s_ref[...] = jnp.zeros(s_ref.shape, F32) if s0_ref is None else s0_ref[0]

    misc_b = _pad_rows(misc_ref[0], rpad).astype(BF)
    pos = lax.broadcasted_iota(I32, (rows, GLA_DK), 0) % chunk
    cums = []
    for hh in range(heads):
        la = jnp.dot(misc_b, wal_ref[hh], preferred_element_type=F32)[:rows]
        cum = _log_sigmoid(la + bal_ref[hh]) / GATE_TAU
        d = 1
        while d < chunk:
            cum = cum + jnp.where(pos >= d, pltpu.roll(cum, d, axis=0), 0.0)
            d *= 2
        cums.append(cum)

    causal = (lax.broadcasted_iota(I32, (cpad, cpad), 0) >= lax.broadcasted_iota(I32, (cpad, cpad), 1))
    gn = gn_ref[...]
    for c in range(nchunks):
        r = slice(c * chunk, (c + 1) * chunk)
        for hh in range(heads):
            kcols = slice(hh * GLA_DK, (hh + 1) * GLA_DK)
            vcols = slice(hh * GLA_DV, (hh + 1) * GLA_DV)
            cum_c = cums[hh][r]
            last = cum_c[chunk - 1:chunk]
            q_c = q_ref[0, r, kcols] * (GLA_DK ** -0.5)
            k_c = k_ref[0, r, kcols]
            v_b = _pad_rows(v_ref[0, r, vcols], cpad).astype(BF)
            qt = _pad_rows(q_c * jnp.exp(cum_c), cpad).astype(BF)
            kt = _pad_rows(k_c * jnp.exp(-cum_c), cpad).astype(BF)
            kd = _pad_rows(k_c * jnp.exp(last - cum_c), cpad).astype(BF)
            att = lax.dot_general(qt, kt, (((1,), (1,)), ((), ())), preferred_element_type=F32)
            att = jnp.where(causal, att, 0.0).astype(BF)
            s_old = s_ref[hh]
            o = (jnp.dot(att, v_b, preferred_element_type=F32)
                 + jnp.dot(qt, s_old.astype(BF), preferred_element_type=F32))[:chunk]
            decay = jnp.transpose(jnp.broadcast_to(jnp.exp(last), (GLA_DK, GLA_DK)))
            upd = lax.dot_general(kd, v_b, (((0,), (0,)), ((), ())), preferred_element_type=F32)
            s_ref[hh] = jnp.concatenate([decay] * (GLA_DV // GLA_DK), axis=1) * s_old + upd
            o = o * lax.rsqrt(jnp.mean(o * o, axis=-1, keepdims=True) + NORM_EPS) * gn
            gate = g_ref[0, r, vcols]
            o_ref[0, r, vcols] = (o * (gate * jax.nn.sigmoid(gate))).astype(o_ref.dtype)

    @pl.when(t == pl.num_programs(2) - 1)
    def _():
        sout_ref[0] = s_ref[...]


def _gla(misc, wal, bal, qb, kb, vb, gb, gn, s0, chunk, nchunks, heads, out_dtype):
    batch, seq, _ = misc.shape
    rows = chunk * nchunks
    head = lambda b, h, t: (b, t, h)
    in_specs = [pl.BlockSpec((1, rows, LANE), lambda b, h, t: (b, t, 0)),
                pl.BlockSpec((heads, LANE, GLA_DK), lambda b, h, t: (h, 0, 0)),
                pl.BlockSpec((heads, 1, GLA_DK), lambda b, h, t: (h, 0, 0)),
                pl.BlockSpec((1, rows, heads * GLA_DK), head), pl.BlockSpec((1, rows, heads * GLA_DK), head),
                pl.BlockSpec((1, rows, heads * GLA_DV), head), pl.BlockSpec((1, rows, heads * GLA_DV), head),
                pl.BlockSpec((1, GLA_DV), lambda b, h, t: (0, 0))]
    args = [misc, wal, bal, qb, kb, vb, gb, gn]
    state_spec = pl.BlockSpec((1, heads, GLA_DK, GLA_DV), lambda b, h, t: (b, h, 0, 0))
    if s0 is not None:
        in_specs.append(state_spec)
        args.append(s0)
    kern = functools.partial(_gla_kernel, chunk=chunk, nchunks=nchunks, heads=heads, has_s0=s0 is not None)
    return pl.pallas_call(
        kern,
        grid=(batch, GLA_HEADS // heads, seq // rows),
        in_specs=in_specs,
        out_specs=[pl.BlockSpec((1, rows, heads * GLA_DV), head), state_spec],
        out_shape=[jax.ShapeDtypeStruct((batch, seq, GLA_DV_TOT), out_dtype),
                   jax.ShapeDtypeStruct((batch, GLA_HEADS, GLA_DK, GLA_DV), F32)],
        scratch_shapes=[pltpu.VMEM((heads, GLA_DK, GLA_DV), F32)],
        compiler_params=pltpu.CompilerParams(dimension_semantics=("parallel", "parallel", "arbitrary"),
                                             vmem_limit_bytes=VMEM_LIMIT),
        name="gla",
    )(*args)


def _layer_norm(x, g, b):
    xc = x - jnp.mean(x, axis=-1, keepdims=True)
    var = jnp.mean(xc * xc, axis=-1, keepdims=True)
    return xc * lax.rsqrt(var + NORM_EPS) * g + b


def _post_kernel(x_ref, attn_ref, ob_ref, ga_ref, gb_ref, wao_ref, wgo_ref, wout_ref, ln1g_ref, ln1b_ref,
                 wf1_ref, wf2_ref, ln2g_ref, ln2b_ref, out_ref, *, ff_chunk):
    branch_a = jnp.dot(attn_ref[...], wao_ref[...], preferred_element_type=F32)
    branch_b = jnp.dot(ob_ref[...], wgo_ref[...], preferred_element_type=F32)
    merged = jax.nn.sigmoid(ga_ref[...]) * branch_a + jax.nn.sigmoid(gb_ref[...]) * branch_b
    mix = jnp.dot(merged.astype(BF), wout_ref[...], preferred_element_type=F32)
    h = _layer_norm(DN_ALPHA * x_ref[...] + mix, ln1g_ref[...], ln1b_ref[...])
    hb = h.astype(BF)
    ff = jnp.zeros(h.shape, F32)
    for c in range(D_FF // ff_chunk):
        cols = slice(c * ff_chunk, (c + 1) * ff_chunk)
        a = jnp.maximum(jnp.dot(hb, wf1_ref[:, cols], preferred_element_type=F32), 0.0)
        ff = ff + jnp.dot((a * a).astype(BF), wf2_ref[cols, :], preferred_element_type=F32)
    out_ref[...] = _layer_norm(DN_ALPHA * h + ff, ln2g_ref[...], ln2b_ref[...])


def _post(x2d, attn, ob, ga, gb, wao, wgo, wout, ln1g, ln1b, wf1, wf2, ln2g, ln2b, tm):
    m = x2d.shape[0]
    row = lambda width: pl.BlockSpec((tm, width), lambda i: (i, 0))
    return pl.pallas_call(
        functools.partial(_post_kernel, ff_chunk=1024),
        grid=(m // tm,),
        in_specs=[row(D_MODEL), row(N_HEADS * HEAD_DIM), row(GLA_DV_TOT), row(D_MODEL), row(D_MODEL),
                  _const_spec(wao.shape), _const_spec(wgo.shape), _const_spec(wout.shape),
                  _const_spec(ln1g.shape), _const_spec(ln1b.shape), _const_spec(wf1.shape), _const_spec(wf2.shape),
                  _const_spec(ln2g.shape), _const_spec(ln2b.shape)],
        out_specs=row(D_MODEL),
        out_shape=jax.ShapeDtypeStruct((m, D_MODEL), F32),
        compiler_params=pltpu.CompilerParams(dimension_semantics=("parallel",), vmem_limit_bytes=VMEM_LIMIT),
        name="post",
    )(x2d, attn, ob, ga, gb, wao, wgo, wout, ln1g, ln1b, wf1, wf2, ln2g, ln2b)


def _dsa_sample_kernel(pt_ref, qi_ref, misc_ref, qa_ref, ka_ref, va_ref,
                       ckidx_ref, ck_ref, cv_ref, out_ref,
                       ibuf, kbuf, vbuf, isem, ksem, vsem, keys_ref, *, t, n_pages, ppc, topk):
    b = pl.program_id(0)
    nch = n_pages // ppc
    kc = ppc * PAGE_SIZE
    past = n_pages * PAGE_SIZE
    hq = IDX_HEADS * t

    def page_copy(src_ref, buf, sem, ch, slot, p):
        return pltpu.make_async_copy(src_ref.at[pt_ref[b, ch * ppc + p]],
                                     buf.at[slot, ..., pl.ds(p * PAGE_SIZE, PAGE_SIZE)], sem.at[slot])

    def start_chunk(src_ref, buf, sem, ch, slot):
        for p in range(ppc):
            page_copy(src_ref, buf, sem, ch, slot, p).start()

    def wait_chunk(src_ref, buf, sem, ch, slot):
        for p in range(ppc):
            page_copy(src_ref, buf, sem, ch, slot, p).wait()

    qi_all = jnp.concatenate([qi_ref[0, :, h * LANE:h * LANE + IDX_DIM] for h in range(IDX_HEADS)],
                             axis=0).astype(BF)
    misc = misc_ref[0]
    w = misc[:, MISC_WI:MISC_WI + IDX_HEADS] * (IDX_HEADS ** -0.5)
    w_all = jnp.concatenate([w[:, h:h + 1] for h in range(IDX_HEADS)], axis=0)
    qpos = past + lax.broadcasted_iota(I32, (t, LANE), 0)

    nt_dims = (((1,), (1,)), ((), ()))

    def index_scores(d):
        r = jnp.maximum(d, 0.0) * w_all
        return jnp.sum(r.reshape(IDX_HEADS, t, r.shape[1]), axis=0)

    start_chunk(ckidx_ref, ibuf, isem, 0, 0)
    for ch in range(nch):
        slot = ch % 2
        if ch + 1 < nch:
            start_chunk(ckidx_ref, ibuf, isem, ch + 1, 1 - slot)
        wait_chunk(ckidx_ref, ibuf, isem, ch, slot)
        d = jnp.dot(qi_all, ibuf[slot].astype(BF), preferred_element_type=F32)
        keys_ref[:, ch * kc:(ch + 1) * kc] = index_scores(d)
    knew = _pad_rows(misc[:, :IDX_DIM], LANE).astype(BF)
    d_new = lax.dot_general(qi_all, knew, nt_dims, preferred_element_type=F32)
    new_pos = past + lax.broadcasted_iota(I32, (t, LANE), 1)
    keys_ref[:, past:past + LANE] = jnp.where(new_pos <= qpos, index_scores(d_new), -jnp.inf)

    start_chunk(ck_ref, kbuf, ksem, 0, 0)
    start_chunk(cv_ref, vbuf, vsem, 0, 0)

    width = past + LANE
    col = lax.broadcasted_iota(I32, (t, width), 1)

    def count_where(indicator):
        return jnp.sum(_lane_fold(indicator(keys_ref[...])), axis=1, keepdims=True)

    thr_key = _radix_max(
        lambda cand: count_where(lambda s: jnp.where(s >= _key_to_f32(cand), 1.0, 0.0)),
        jnp.full((t, 1), INT_MIN, I32), 32, topk)
    thr = _key_to_f32(jnp.maximum(thr_key, KEY_LOWEST_FINITE))
    n_ge = count_where(lambda s: jnp.where(s >= thr, 1.0, 0.0))
    surplus = n_ge > topk

    @pl.when(jnp.max(jnp.where(surplus, 1, 0)) > 0)
    def _():
        need = topk - count_where(lambda s: jnp.where(s > thr, 1.0, 0.0))
        nbits = max(1, (width - 1).bit_length())
        cut = _tie_cut(lambda cand: count_where(
            lambda s: jnp.where(s == thr, jnp.where(col < cand, 1.0, 0.0), 0.0)), need, t, nbits)
        s = keys_ref[...]
        dropped = jnp.where(s == thr, jnp.where(col > cut, jnp.where(surplus, 1, 0), 0), 0)
        keys_ref[...] = jnp.where(dropped > 0, -jnp.inf, s)

    gt = GQA_GROUP * t
    qg = [jnp.concatenate([qa_ref[0, :, h * LANE + g * HEAD_DIM:h * LANE + (g + 1) * HEAD_DIM]
                           for h in range(g * GQA_GROUP, (g + 1) * GQA_GROUP)], axis=0).astype(BF)
          for g in range(N_KV_HEADS)]

    def attend(carry, scores, values, sel):
        m_old, l_old, acc_old = carry
        s = jnp.concatenate([scores(g) for g in range(N_KV_HEADS)], axis=0)
        n = s.shape[1]
        s = jnp.where(sel[None], s.reshape(N_HEADS, t, n), -jnp.inf).reshape(N_HEADS * t, n)
        m_new = jnp.maximum(m_old, jnp.max(s, axis=1, keepdims=True))
        alpha = jnp.exp2(m_old - m_new)
        p = jnp.exp2(s - m_new)
        l_new = alpha * l_old + jnp.sum(p, axis=1, keepdims=True)
        pb = p.astype(BF)
        pv = jnp.concatenate([values(g, pb[g * gt:(g + 1) * gt]) for g in range(N_KV_HEADS)], axis=0)
        return m_new, l_new, alpha * acc_old + pv

    carry = (jnp.full((hq, 1), NEG_BIG, F32), jnp.zeros((hq, 1), F32), jnp.zeros((hq, HEAD_DIM), F32))
    for ch in range(nch):
        slot = ch % 2
        if ch + 1 < nch:
            start_chunk(ck_ref, kbuf, ksem, ch + 1, 1 - slot)
            start_chunk(cv_ref, vbuf, vsem, ch + 1, 1 - slot)
        wait_chunk(ck_ref, kbuf, ksem, ch, slot)
        wait_chunk(cv_ref, vbuf, vsem, ch, slot)
        sel = keys_ref[:, ch * kc:(ch + 1) * kc] >= thr
        carry = attend(
            carry,
            lambda g: jnp.dot(qg[g], kbuf[slot, g].astype(BF), preferred_element_type=F32),
            lambda g, pb: lax.dot_general(pb, vbuf[slot, g].astype(BF), nt_dims, preferred_element_type=F32),
            sel)
    sel_new = keys_ref[:, past:past + LANE] >= thr
    k_new = _pad_rows(ka_ref[0], LANE)
    v_new = _pad_rows(va_ref[0], LANE)
    carry = attend(
        carry,
        lambda g: lax.dot_general(qg[g], k_new[:, g * HEAD_DIM:(g + 1) * HEAD_DIM].astype(BF), nt_dims,
                                  preferred_element_type=F32),
        lambda g, pb: jnp.dot(pb, v_new[:, g * HEAD_DIM:(g + 1) * HEAD_DIM].astype(BF), preferred_element_type=F32),
        sel_new)
    _, l_fin, acc_fin = carry
    o = acc_fin / l_fin
    for h in range(N_HEADS):
        out_ref[0, :, h * HEAD_DIM:(h + 1) * HEAD_DIM] = o[h * t:(h + 1) * t].astype(out_ref.dtype)


def _dsa_sample(page_table, qi, misc, qa, ka, va, cache_kidx, cache_k, cache_v, ppc):
    batch, t, _ = misc.shape
    n_pages = page_table.shape[1]
    past = n_pages * PAGE_SIZE
    topk = min(TOPK_MAX, (past + t) // 4)
    kern = functools.partial(_dsa_sample_kernel, t=t, n_pages=n_pages, ppc=ppc, topk=topk)
    row = lambda width: pl.BlockSpec((1, t, width), lambda b, pt: (b, 0, 0))
    hbm = pl.BlockSpec(memory_space=pl.ANY)
    grid_spec = pltpu.PrefetchScalarGridSpec(
        num_scalar_prefetch=1,
        grid=(batch,),
        in_specs=[row(IDX_HEADS * LANE), row(LANE), row(N_HEADS * LANE), row(LANE), row(LANE),
                  hbm, hbm, hbm],
        out_specs=row(N_HEADS * HEAD_DIM),
        scratch_shapes=[pltpu.VMEM((2, IDX_DIM, ppc * PAGE_SIZE), F32),
                        pltpu.VMEM((2, N_KV_HEADS, HEAD_DIM, ppc * PAGE_SIZE), F32),
                        pltpu.VMEM((2, N_KV_HEADS, HEAD_DIM, ppc * PAGE_SIZE), F32),
                        pltpu.SemaphoreType.DMA((2,)), pltpu.SemaphoreType.DMA((2,)), pltpu.SemaphoreType.DMA((2,)),
                        pltpu.VMEM((t, past + LANE), F32)],
    )
    return pl.pallas_call(
        kern,
        grid_spec=grid_spec,
        out_shape=jax.ShapeDtypeStruct((batch, t, N_HEADS * HEAD_DIM), F32),
        compiler_params=pltpu.CompilerParams(dimension_semantics=("arbitrary",), vmem_limit_bytes=VMEM_LIMIT),
        name="dsa_sample",
    )(page_table, qi, misc, qa, ka, va, cache_kidx, cache_k, cache_v)


def _row_tile(m, cap):
    tm = min(m, cap)
    while m % tm:
        tm //= 2
    return tm


def _gla_and_post(x2d, attn, misc, gla_in, gates, weights, batch, seq, gla_s0, gla_chunk, gla_nchunks, gla_heads,
                  gla_out_dtype, tm):
    (_, wal, bal, gn, wao, wgo, wout, ln1g, ln1b, wf1, wf2, ln2g, ln2b) = weights
    b3 = lambda a: a.reshape(batch, seq, a.shape[-1])
    qb, kb, vb, gb = gla_in
    ob, s_new = _gla(b3(misc), wal, bal, b3(qb), b3(kb), b3(vb), b3(gb), gn, gla_s0,
                     gla_chunk, gla_nchunks, gla_heads, gla_out_dtype)
    ob = ob.reshape(batch * seq, GLA_DV_TOT).astype(BF)
    y = _post(x2d, attn, ob, *gates, wao, wgo, wout, ln1g, ln1b, wf1, wf2, ln2g, ln2b, tm)
    return y, s_new


def kernel(x_prompt, x_sample, cache_k, cache_v, cache_kidx, state_gla, page_table,
           w_in, w_alpha2, b_alpha, gla_norm_g, w_attn_o, w_gla_o, w_out,
           ln1_g, ln1_b, w_ff1, w_ff2, ln2_g, ln2_b):
    assert w_in.shape[0] == DEPTH == 1
    bp, tp, _ = x_prompt.shape
    bs, ts, _ = x_sample.shape
    n_pool = cache_k.shape[1]
    l = 0
    wal = jnp.zeros((LANE, GLA_DK_TOT), F32).at[MISC_AB:MISC_AB + GATE_RANK].set(w_alpha2[l])
    wal = wal.reshape(LANE, GLA_HEADS, GLA_DK).transpose(1, 0, 2).astype(BF)
    weights = (_pack_w_in(w_in[l]), wal, b_alpha[l].reshape(GLA_HEADS, 1, GLA_DK), gla_norm_g[l].reshape(1, GLA_DV),
               w_attn_o[l].astype(BF), w_gla_o[l].astype(BF), w_out[l].astype(BF),
               ln1_g[l].reshape(1, D_MODEL), ln1_b[l].reshape(1, D_MODEL),
               w_ff1[l].astype(BF), w_ff2[l].astype(BF), ln2_g[l].reshape(1, D_MODEL), ln2_b[l].reshape(1, D_MODEL))

    xp = x_prompt.reshape(bp * tp, D_MODEL)
    kc = min(512, tp)
    tq = min(256, tp)
    (qa, ka_t, va_t, kab_t, vab, qi, misc, ki_t, miscb_t, *rest) = _proj(xp, weights[0], kc, token_minor_seq=tp)
    attn = _dsa_prompt(qi, misc, miscb_t, qa, kab_t, vab, batch=bp, seq=tp, tq=tq, ts=min(128, tq), kc=kc)
    gchunk = min(GLA_CHUNK, tp)
    y_p, s_p = _gla_and_post(xp, attn, misc, rest[:4], rest[4:], weights, bp, tp, None, gchunk,
                             min(8, tp // gchunk), GLA_HEADS, BF, kc)

    def kv_t(a):
        return jnp.transpose(a.reshape(bp, N_KV_HEADS, HEAD_DIM, tp), (0, 3, 1, 2))[None]

    xs = x_sample.reshape(bs * ts, D_MODEL)
    tm_s = _row_tile(bs * ts, 512)
    (qa, ka, va, qi, misc, *rest) = _proj(xs, weights[0], tm_s)
    ckidx = jnp.transpose(cache_kidx[l], (0, 2, 1))
    ck = jnp.transpose(cache_k[l], (0, 2, 3, 1))
    cv = jnp.transpose(cache_v[l], (0, 2, 3, 1))
    f3 = lambda a: a.astype(F32).reshape(bs, ts, a.shape[-1])
    attn = _dsa_sample(page_table, f3(qi), f3(misc), f3(qa), f3(ka), f3(va), ckidx, ck, cv,
                       ppc=min(16, page_table.shape[1]))
    attn = attn.reshape(bs * ts, N_HEADS * HEAD_DIM).astype(BF)
    y_s, s_s = _gla_and_post(xs, attn, misc, rest[:4], rest[4:], weights, bs, ts, state_gla[l], ts, 1, GLA_HEADS,
                             F32, tm_s)

    def kv(a):
        return a.reshape(1, bs, ts, N_KV_HEADS, HEAD_DIM)

    return (y_p.reshape(bp, tp, D_MODEL), y_s.reshape(bs, ts, D_MODEL),
            kv_t(ka_t), kv_t(va_t), jnp.transpose(ki_t, (0, 2, 1))[None], s_p[None],
            kv(ka), kv(va), misc[:, :IDX_DIM].reshape(1, bs, ts, IDX_DIM), s_s[None])
```

```python
import functools

import jax
import jax.numpy as jnp
from jax import lax
from jax.experimental import pallas as pl
from jax.experimental.pallas import tpu as pltpu

D_MODEL = 1024
N_HEADS = 8
N_KV_HEADS = 2
HEAD_DIM = 64
GQA_GROUP = N_HEADS // N_KV_HEADS
IDX_HEADS = 8
IDX_DIM = 64
TOPK_MAX = 256
PAGE_SIZE = 128
GLA_HEADS = 4
GLA_DK_TOT = D_MODEL // 2
GLA_DV_TOT = D_MODEL
GLA_DK = GLA_DK_TOT // GLA_HEADS
GLA_DV = GLA_DV_TOT // GLA_HEADS
GATE_RANK = 16
GATE_TAU = 16.0
GLA_CHUNK = 64
D_FF = 4 * D_MODEL
NORM_EPS = 1e-5
DEPTH = 1
DN_ALPHA = (2 * DEPTH) ** 0.25

LANE = 128
SUBLANE = 8
VMEM_LIMIT = 56 * 1024 * 1024
BF = jnp.bfloat16
F32 = jnp.float32
I32 = jnp.int32
I16 = jnp.int16
INT_MIN = -2 ** 31
I16_MIN = -2 ** 15
LOG2_E = 1.4426950408889634
KEY_LOWEST_FINITE = -0x7F7FFFFF
NEG_BIG = -1e30

MISC_WI = IDX_DIM
MISC_AB = IDX_DIM + IDX_HEADS

_PROJ_GROUPS = (
    ('qa', N_HEADS * LANE), ('ka', LANE), ('va', LANE), ('qi', IDX_HEADS * LANE), ('misc', LANE),
    ('qb', GLA_DK_TOT), ('kb', GLA_DK_TOT), ('vb', GLA_DV_TOT), ('gb', GLA_DV_TOT),
    ('ga', D_MODEL), ('gb2', D_MODEL),
)
_PROJ_OFF = {}
_o = 0
for _n, _w in _PROJ_GROUPS:
    _PROJ_OFF[_n] = (_o, _o + _w)
    _o += _w
PROJ_COLS = _o


def _const_spec(shape):
    nd = len(shape)
    return pl.BlockSpec(shape, lambda *_: (0,) * nd, pipeline_mode=pl.Buffered(1))


def _pack_w_in(w_in):
    off = 0
    parts = {}
    for name, width in (('q_a', 512), ('k_a', 128), ('v_a', 128), ('q_i', 512), ('k_i', 64), ('w_i', 8),
                        ('q_b', 512), ('k_b', 512), ('v_b', 1024), ('g_b', 1024), ('a_b', 16),
                        ('gate_a', 1024), ('gate_b', 1024)):
        parts[name] = w_in[:, off:off + width]
        off += width
    z64 = jnp.zeros((D_MODEL, HEAD_DIM), F32)
    qa_cols = []
    for h in range(N_HEADS):
        q = parts['q_a'][:, h * HEAD_DIM:(h + 1) * HEAD_DIM] * (HEAD_DIM ** -0.5 * LOG2_E)
        qa_cols += [q, z64] if h // GQA_GROUP == 0 else [z64, q]
    qi_cols = []
    for h in range(IDX_HEADS):
        qi_cols += [parts['q_i'][:, h * IDX_DIM:(h + 1) * IDX_DIM] * (IDX_DIM ** -0.5), z64]
    misc = jnp.concatenate([parts['k_i'], parts['w_i'], parts['a_b'],
                            jnp.zeros((D_MODEL, LANE - MISC_AB - GATE_RANK), F32)], axis=1)
    cols = qa_cols + [parts['k_a'], parts['v_a']] + qi_cols + [
        misc, parts['q_b'], parts['k_b'], parts['v_b'], parts['g_b'], parts['gate_a'], parts['gate_b']]
    return jnp.concatenate(cols, axis=1).astype(BF)


def _proj_kernel(x_ref, w_ref, *out_refs, token_minor):
    xb = x_ref[...].astype(BF)

    def mm(name):
        a, b = _PROJ_OFF[name]
        return jnp.dot(xb, w_ref[:, a:b], preferred_element_type=F32)

    if token_minor:
        (qa_ref, kat_ref, vat_ref, kabt_ref, vab_ref, qi_ref, misc_ref, kit_ref, miscbt_ref,
         qb_ref, kb_ref, vb_ref, gb_ref, ga_ref, gb2_ref, ka_tmp, va_tmp) = out_refs
    else:
        (qa_ref, ka_ref, va_ref, qi_ref, misc_ref, qb_ref, kb_ref, vb_ref, gb_ref, ga_ref, gb2_ref) = out_refs
    qa_ref[...] = mm('qa').astype(BF)
    qi_ref[...] = mm('qi').astype(BF)
    ka, va, misc = mm('ka'), mm('va'), mm('misc')
    misc_ref[...] = misc
    if token_minor:
        ka_tmp[...] = ka
        va_tmp[...] = va
        ka_t, va_t, misc_t = ka_tmp[...].T, va_tmp[...].T, misc_ref[...].T
        kat_ref[0] = ka_t
        vat_ref[0] = va_t
        kit_ref[0] = misc_t[:IDX_DIM]
        kabt_ref[0] = ka_t.astype(BF)
        miscbt_ref[0] = misc_t.astype(BF)
        vab_ref[...] = va.astype(BF)
    else:
        ka_ref[...] = ka
        va_ref[...] = va
    qb_ref[...] = mm('qb')
    kb_ref[...] = mm('kb')
    vb_ref[...] = mm('vb')
    gb_ref[...] = mm('gb')
    ga_ref[...] = mm('ga')
    gb2_ref[...] = mm('gb2')


def _proj(x2d, w_packed, tm, token_minor_seq=None):
    m = x2d.shape[0]
    widths = dict(_PROJ_GROUPS)
    row = lambda n, dt: (pl.BlockSpec((tm, widths[n]), lambda i: (i, 0)), jax.ShapeDtypeStruct((m, widths[n]), dt))
    if token_minor_seq is None:
        outs = [row('qa', BF), row('ka', F32), row('va', F32), row('qi', BF), row('misc', F32)]
    else:
        nt = token_minor_seq // tm
        seq_t = lambda rows: (pl.BlockSpec((1, rows, tm), lambda i: (i // nt, 0, i % nt)),
                              jax.ShapeDtypeStruct((m // token_minor_seq, rows, token_minor_seq), F32))
        chunk_t = (pl.BlockSpec((1, LANE, tm), lambda i: (i, 0, 0)), jax.ShapeDtypeStruct((m // tm, LANE, tm), BF))
        outs = [row('qa', BF), seq_t(LANE), seq_t(LANE), chunk_t, row('va', BF), row('qi', BF), row('misc', F32),
                seq_t(IDX_DIM), chunk_t]
    outs += [row(n, F32) for n in ('qb', 'kb', 'vb', 'gb', 'ga', 'gb2')]
    return pl.pallas_call(
        functools.partial(_proj_kernel, token_minor=token_minor_seq is not None),
        grid=(m // tm,),
        in_specs=[pl.BlockSpec((tm, D_MODEL), lambda i: (i, 0)), _const_spec((D_MODEL, PROJ_COLS))],
        out_specs=[spec for spec, _ in outs],
        out_shape=[shape for _, shape in outs],
        scratch_shapes=[] if token_minor_seq is None else [pltpu.VMEM((tm, LANE), F32)] * 2,
        compiler_params=pltpu.CompilerParams(dimension_semantics=("parallel",), vmem_limit_bytes=VMEM_LIMIT),
        name="proj",
    )(x2d, w_packed)


def _key_to_f32(key):
    return pltpu.bitcast(jnp.where(key < 0, INT_MIN - key, key), F32)


def _tree_reduce(op, xs):
    xs = list(xs)
    while len(xs) > 1:
        xs = [op(xs[j], xs[j + 1]) if j + 1 < len(xs) else xs[j] for j in range(0, len(xs), 2)]
    return xs[0]


def _lane_fold(x, op=jnp.add):
    return _tree_reduce(op, [x[:, j * LANE:(j + 1) * LANE] for j in range(x.shape[1] // LANE)])


def _radix_max(count_ge, start, nbits, k):
    def body(it, ans):
        cand = ans + jnp.left_shift(jnp.int32(1), nbits - 1 - it)
        return jnp.where(count_ge(cand) >= k, cand, ans)
    return lax.fori_loop(0, nbits, body, start)


def _tie_cut(count_tie_below, need, rows, nbits):
    def body(it, m):
        cand = m | jnp.left_shift(jnp.int32(1), nbits - 1 - it)
        return jnp.where(count_tie_below(cand) < need, cand, m)
    return lax.fori_loop(0, nbits, body, jnp.zeros((rows, 1), I32))


def _dsa_prompt_kernel(qi_ref, misc_ref, kibt_ref, qa_ref, kabt_ref, vab_ref, out_ref,
                       sc_ref, hi_ref, lo_ref, s_ref, thr_ref, *, tq, ts, kc, topk, seq):
    i = pl.program_id(1)
    nkc = ((i + 1) * tq + kc - 1) // kc
    lane_pos = lax.broadcasted_iota(I32, (tq, kc), 1)

    def head_stack(ref, rows, nheads):
        return jnp.concatenate([ref[rows, h * LANE:(h + 1) * LANE] for h in range(nheads)], axis=0)

    qi_stack = [head_stack(qi_ref, slice(r * ts, (r + 1) * ts), IDX_HEADS) for r in range(tq // ts)]

    def score_chunk(c, carry):
        kchunk_t = kibt_ref[c]
        for r in range(tq // ts):
            rows = slice(r * ts, (r + 1) * ts)
            w = misc_ref[rows, MISC_WI:MISC_WI + IDX_HEADS] * (IDX_HEADS ** -0.5)
            d = jnp.dot(qi_stack[r], kchunk_t, preferred_element_type=F32)
            acc = jnp.zeros((ts, kc), F32)
            for h in range(IDX_HEADS):
                acc = acc + jnp.maximum(d[h * ts:(h + 1) * ts], 0.0) * w[:, h:h + 1]
            tpos = i * tq + r * ts + lax.broadcasted_iota(I32, (ts, kc), 0)
            score = jnp.where(c * kc + lax.broadcasted_iota(I32, (ts, kc), 1) <= tpos, acc, -jnp.inf)
            sc_ref[c, rows, :] = score
            score_t = score.T
            lo_ref[c, :, rows] = score_t
            hi_ref[c, :, rows] = score_t.astype(BF)
        return carry
    lax.fori_loop(0, nkc, score_chunk, 0)

    def count_where(indicator):
        def body(c, cnt):
            return cnt + _lane_fold(indicator(c, sc_ref[c]))
        cnt = lax.fori_loop(0, nkc, body, jnp.zeros((tq, LANE), F32))
        return jnp.sum(cnt, axis=1, keepdims=True)

    def count_keys(ref, pack, zero, indicator):
        def body(c, cnt):
            ind = indicator(ref[c].reshape(kc // pack, pack, tq))
            return cnt + _tree_reduce(jnp.add, [ind[j] for j in range(kc // pack)])
        cnt = lax.fori_loop(0, nkc, body, jnp.zeros((pack, tq), zero.dtype)).astype(F32)
        cnt = _tree_reduce(jnp.add, [cnt[j * SUBLANE:(j + 1) * SUBLANE] for j in range(pack // SUBLANE)])
        for shift in (4, 2, 1):
            cnt = cnt + pltpu.roll(cnt, shift, axis=0)
        return cnt

    def bf16_rows(key16):
        bits = jnp.where(key16 < 0, I16_MIN - key16, key16)
        val = pltpu.bitcast(jnp.left_shift(bits, 16), F32)
        return jnp.concatenate([val, val], axis=0).astype(BF)[None]

    one16, zero16 = jnp.int16(1), jnp.int16(0)
    thr_hi = _radix_max(
        lambda cand: count_keys(hi_ref, 2 * SUBLANE, zero16, lambda v: jnp.where(v >= bf16_rows(cand), one16, zero16)),
        jnp.full((SUBLANE, tq), I16_MIN, I32), 16, float(topk))
    thr_key = _radix_max(
        lambda cand: count_keys(lo_ref, SUBLANE, jnp.float32(0),
                                lambda v: jnp.where(v >= _key_to_f32(cand)[None], 1.0, 0.0)),
        jnp.left_shift(thr_hi - 1, 16), 17, float(topk))
    thr_t = _key_to_f32(jnp.maximum(thr_key, KEY_LOWEST_FINITE))
    thr_rows = jnp.transpose(jnp.broadcast_to(thr_t[0:1], (LANE, tq)))
    thr = thr_rows[:, 0:1]
    n_ge = count_where(lambda c, s: jnp.where(s >= thr, 1.0, 0.0))
    surplus = jnp.where(n_ge > topk, 1, 0)

    @pl.when(jnp.max(surplus) > 0)
    def _():
        need = topk - count_where(lambda c, s: jnp.where(s > thr, 1.0, 0.0))
        nbits = max(1, (seq - 1).bit_length())
        cut = _tie_cut(lambda cand: count_where(
            lambda c, s: jnp.where(s == thr, jnp.where(c * kc + lane_pos < cand, 1.0, 0.0), 0.0)), need, tq, nbits)

        def drop(c, carry):
            s = sc_ref[c]
            dropped = jnp.where(s == thr, jnp.where(c * kc + lane_pos > cut, surplus, 0), 0)
            sc_ref[c] = jnp.where(dropped > 0, -jnp.inf, s)
            return carry
        lax.fori_loop(0, nkc, drop, 0)

    v_lanes = lax.broadcasted_iota(I32, (kc, LANE), 1) // HEAD_DIM
    thr_ref[...] = jnp.broadcast_to(thr, (tq, LANE))
    for r in range(tq // ts):
        rows = slice(r * ts, (r + 1) * ts)
        nkc_r = (i * tq + (r + 1) * ts + kc - 1) // kc
        thr_r = thr_ref[rows, 0:1]
        qa_stack = head_stack(qa_ref, rows, N_HEADS)

        def score_pass(c, mrun):
            sel = sc_ref[c, rows, :] >= thr_r
            s = jnp.dot(qa_stack, kabt_ref[c], preferred_element_type=F32)
            s = jnp.where(sel[None], s.reshape(N_HEADS, ts, kc), -jnp.inf).reshape(N_HEADS * ts, kc)
            out = []
            for g in range(N_KV_HEADS):
                s_g = s[g * GQA_GROUP * ts:(g + 1) * GQA_GROUP * ts]
                s_ref[c, g] = s_g
                out.append(jnp.maximum(mrun[g], _lane_fold(s_g, jnp.maximum)))
            return tuple(out)
        mrun = lax.fori_loop(0, nkc_r, score_pass,
                             tuple(jnp.full((GQA_GROUP * ts, LANE), -jnp.inf, F32) for _ in range(N_KV_HEADS)))
        m = [jnp.max(mrun[g], axis=1, keepdims=True) for g in range(N_KV_HEADS)]

        def value_pass(c, acc):
            vchunk = vab_ref[pl.ds(pl.multiple_of(c * kc, kc), kc), :]
            out = []
            for g in range(N_KV_HEADS):
                vaug = jnp.where(v_lanes == g, vchunk, jnp.ones_like(vchunk))
                p = jnp.exp2(s_ref[c, g] - m[g]).astype(BF)
                out.append(acc[g] + jnp.dot(p, vaug, preferred_element_type=F32))
            return tuple(out)
        acc = lax.fori_loop(0, nkc_r, value_pass,
                            tuple(jnp.zeros((GQA_GROUP * ts, LANE), F32) for _ in range(N_KV_HEADS)))

        for g in range(N_KV_HEADS):
            ones_lane = (1 - g) * HEAD_DIM
            o = acc[g] / acc[g][:, ones_lane:ones_lane + 1]
            for j in range(GQA_GROUP):
                h = g * GQA_GROUP + j
                out_ref[rows, h * HEAD_DIM:(h + 1) * HEAD_DIM] = (
                    o[j * ts:(j + 1) * ts, g * HEAD_DIM:(g + 1) * HEAD_DIM].astype(out_ref.dtype))


def _dsa_prompt(qi, misc, miscb_t, qa, kab_t, vab, batch, seq, tq, ts, kc):
    topk = min(TOPK_MAX, seq // 4)
    nq = seq // tq
    kern = functools.partial(_dsa_prompt_kernel, tq=tq, ts=ts, kc=kc, topk=topk, seq=seq)
    qrow = lambda b, i: (b * nq + i, 0)
    brow = lambda b, i: (b, 0)
    bchunks = pl.BlockSpec((seq // kc, LANE, kc), lambda b, i: (b, 0, 0))
    return pl.pallas_call(
        kern,
        grid=(batch, nq),
        in_specs=[pl.BlockSpec((tq, IDX_HEADS * LANE), qrow), pl.BlockSpec((tq, LANE), qrow),
                  bchunks, pl.BlockSpec((tq, N_HEADS * LANE), qrow),
                  bchunks, pl.BlockSpec((seq, LANE), brow)],
        out_specs=pl.BlockSpec((tq, N_HEADS * HEAD_DIM), qrow),
        out_shape=jax.ShapeDtypeStruct((batch * seq, N_HEADS * HEAD_DIM), BF),
        scratch_shapes=[pltpu.VMEM((seq // kc, tq, kc), F32),
                        pltpu.VMEM((seq // kc, kc, tq), BF),
                        pltpu.VMEM((seq // kc, kc, tq), F32),
                        pltpu.VMEM((seq // kc, N_KV_HEADS, GQA_GROUP * ts, kc), F32),
                        pltpu.VMEM((tq, LANE), F32)],
        compiler_params=pltpu.CompilerParams(dimension_semantics=("parallel", "arbitrary"),
                                             vmem_limit_bytes=VMEM_LIMIT),
        name="dsa_prompt",
    )(qi, misc, miscb_t, qa, kab_t, vab)


def _log_sigmoid(x):
    return jnp.minimum(x, 0.0) - jnp.log1p(jnp.exp(-jnp.abs(x)))


def _pad_rows(x, n):
    if x.shape[0] == n:
        return x
    return jnp.concatenate([x, jnp.zeros((n - x.shape[0], x.shape[1]), x.dtype)], axis=0)


def _gla_kernel(*refs, chunk, nchunks, heads, has_s0):
    if has_s0:
        (misc_ref, wal_ref, bal_ref, q_ref, k_ref, v_ref, g_ref, gn_ref, s0_ref, o_ref, sout_ref, s_ref) = refs
    else:
        (misc_ref, wal_ref, bal_ref, q_ref, k_ref, v_ref, g_ref, gn_ref, o_ref, sout_ref, s_ref) = refs
        s0_ref = None
    t = pl.program_id(2)
    rows = chunk * nchunks
    cpad = max(chunk, 2 * SUBLANE)
    rpad = max(rows, 2 * SUBLANE)

    @pl.when(t == 0)
    def _():
        s_ref[...] = jnp.zeros(s_ref.shape, F32) if s0_ref is None else s0_ref[0]

    misc_b = _pad_rows(misc_ref[0], rpad).astype(BF)
    pos = lax.broadcasted_iota(I32, (rows, GLA_DK), 0) % chunk
    cums = []
    for hh in range(heads):
        la = jnp.dot(misc_b, wal_ref[hh], preferred_element_type=F32)[:rows]
        cum = _log_sigmoid(la + bal_ref[hh]) / GATE_TAU
        d = 1
        while d < chunk:
            cum = cum + jnp.where(pos >= d, pltpu.roll(cum, d, axis=0), 0.0)
            d *= 2
        cums.append(cum)

    causal = (lax.broadcasted_iota(I32, (cpad, cpad), 0) >= lax.broadcasted_iota(I32, (cpad, cpad), 1))
    gn = gn_ref[...]
    for c in range(nchunks):
        r = slice(c * chunk, (c + 1) * chunk)
        for hh in range(heads):
            kcols = slice(hh * GLA_DK, (hh + 1) * GLA_DK)
            vcols = slice(hh * GLA_DV, (hh + 1) * GLA_DV)
            cum_c = cums[hh][r]
            last = cum_c[chunk - 1:chunk]
            q_c = q_ref[0, r, kcols] * (GLA_DK ** -0.5)
            k_c = k_ref[0, r, kcols]
            v_b = _pad_rows(v_ref[0, r, vcols], cpad).astype(BF)
            qt = _pad_rows(q_c * jnp.exp(cum_c), cpad).astype(BF)
            kt = _pad_rows(k_c * jnp.exp(-cum_c), cpad).astype(BF)
            kd = _pad_rows(k_c * jnp.exp(last - cum_c), cpad).astype(BF)
            att = lax.dot_general(qt, kt, (((1,), (1,)), ((), ())), preferred_element_type=F32)
            att = jnp.where(causal, att, 0.0).astype(BF)
            s_old = s_ref[hh]
            o = (jnp.dot(att, v_b, preferred_element_type=F32)
                 + jnp.dot(qt, s_old.astype(BF), preferred_element_type=F32))[:chunk]
            decay = jnp.transpose(jnp.broadcast_to(jnp.exp(last), (GLA_DK, GLA_DK)))
            upd = lax.dot_general(kd, v_b, (((0,), (0,)), ((), ())), preferred_element_type=F32)
            s_ref[hh] = jnp.concatenate([decay] * (GLA_DV // GLA_DK), axis=1) * s_old + upd
            o = o * lax.rsqrt(jnp.mean(o * o, axis=-1, keepdims=True) + NORM_EPS) * gn
            gate = g_ref[0, r, vcols]
            o_ref[0, r, vcols] = (o * (gate * jax.nn.sigmoid(gate))).astype(o_ref.dtype)

    @pl.when(t == pl.num_programs(2) - 1)
    def _():
        sout_ref[0] = s_ref[...]


def _gla(misc, wal, bal, qb, kb, vb, gb, gn, s0, chunk, nchunks, heads, out_dtype):
    batch, seq, _ = misc.shape
    rows = chunk * nchunks
    head = lambda b, h, t: (b, t, h)
    in_specs = [pl.BlockSpec((1, rows, LANE), lambda b, h, t: (b, t, 0)),
                pl.BlockSpec((heads, LANE, GLA_DK), lambda b, h, t: (h, 0, 0)),
                pl.BlockSpec((heads, 1, GLA_DK), lambda b, h, t: (h, 0, 0)),
                pl.BlockSpec((1, rows, heads * GLA_DK), head), pl.BlockSpec((1, rows, heads * GLA_DK), head),
                pl.BlockSpec((1, rows, heads * GLA_DV), head), pl.BlockSpec((1, rows, heads * GLA_DV), head),
                pl.BlockSpec((1, GLA_DV), lambda b, h, t: (0, 0))]
    args = [misc, wal, bal, qb, kb, vb, gb, gn]
    state_spec = pl.BlockSpec((1, heads, GLA_DK, GLA_DV), lambda b, h, t: (b, h, 0, 0))
    if s0 is not None:
        in_specs.append(state_spec)
        args.append(s0)
    kern = functools.partial(_gla_kernel, chunk=chunk, nchunks=nchunks, heads=heads, has_s0=s0 is not None)
    return pl.pallas_call(
        kern,
        grid=(batch, GLA_HEADS // heads, seq // rows),
        in_specs=in_specs,
        out_specs=[pl.BlockSpec((1, rows, heads * GLA_DV), head), state_spec],
        out_shape=[jax.ShapeDtypeStruct((batch, seq, GLA_DV_TOT), out_dtype),
                   jax.ShapeDtypeStruct((batch, GLA_HEADS, GLA_DK, GLA_DV), F32)],
        scratch_shapes=[pltpu.VMEM((heads, GLA_DK, GLA_DV), F32)],
        compiler_params=pltpu.CompilerParams(dimension_semantics=("parallel", "parallel", "arbitrary"),
                                             vmem_limit_bytes=VMEM_LIMIT),
        name="gla",
    )(*args)


def _layer_norm(x, g, b):
    xc = x - jnp.mean(x, axis=-1, keepdims=True)
    var = jnp.mean(xc * xc, axis=-1, keepdims=True)
    return xc * lax.rsqrt(var + NORM_EPS) * g + b


def _post_kernel(x_ref, attn_ref, ob_ref, ga_ref, gb_ref, wao_ref, wgo_ref, wout_ref, ln1g_ref, ln1b_ref,
                 wf1_ref, wf2_ref, ln2g_ref, ln2b_ref, out_ref, *, ff_chunk):
    branch_a = jnp.dot(attn_ref[...], wao_ref[...], preferred_element_type=F32)
    branch_b = jnp.dot(ob_ref[...], wgo_ref[...], preferred_element_type=F32)
    merged = jax.nn.sigmoid(ga_ref[...]) * branch_a + jax.nn.sigmoid(gb_ref[...]) * branch_b
    mix = jnp.dot(merged.astype(BF), wout_ref[...], preferred_element_type=F32)
    h = _layer_norm(DN_ALPHA * x_ref[...] + mix, ln1g_ref[...], ln1b_ref[...])
    hb = h.astype(BF)
    ff = jnp.zeros(h.shape, F32)
    for c in range(D_FF // ff_chunk):
        cols = slice(c * ff_chunk, (c + 1) * ff_chunk)
        a = jnp.maximum(jnp.dot(hb, wf1_ref[:, cols], preferred_element_type=F32), 0.0)
        ff = ff + jnp.dot((a * a).astype(BF), wf2_ref[cols, :], preferred_element_type=F32)
    out_ref[...] = _layer_norm(DN_ALPHA * h + ff, ln2g_ref[...], ln2b_ref[...])


def _post(x2d, attn, ob, ga, gb, wao, wgo, wout, ln1g, ln1b, wf1, wf2, ln2g, ln2b, tm):
    m = x2d.shape[0]
    row = lambda width: pl.BlockSpec((tm, width), lambda i: (i, 0))
    return pl.pallas_call(
        functools.partial(_post_kernel, ff_chunk=1024),
        grid=(m // tm,),
        in_specs=[row(D_MODEL), row(N_HEADS * HEAD_DIM), row(GLA_DV_TOT), row(D_MODEL), row(D_MODEL),
                  _const_spec(wao.shape), _const_spec(wgo.shape), _const_spec(wout.shape),
                  _const_spec(ln1g.shape), _const_spec(ln1b.shape), _const_spec(wf1.shape), _const_spec(wf2.shape),
                  _const_spec(ln2g.shape), _const_spec(ln2b.shape)],
        out_specs=row(D_MODEL),
        out_shape=jax.ShapeDtypeStruct((m, D_MODEL), F32),
        compiler_params=pltpu.CompilerParams(dimension_semantics=("parallel",), vmem_limit_bytes=VMEM_LIMIT),
        name="post",
    )(x2d, attn, ob, ga, gb, wao, wgo, wout, ln1g, ln1b, wf1, wf2, ln2g, ln2b)


def _dsa_sample_kernel(pt_ref, qi_ref, misc_ref, qa_ref, ka_ref, va_ref,
                       ckidx_ref, ck_ref, cv_ref, out_ref,
                       ibuf, kbuf, vbuf, isem, ksem, vsem, keys_ref, *, t, n_pages, ppc, topk):
    b = pl.program_id(0)
    nch = n_pages // ppc
    kc = ppc * PAGE_SIZE
    past = n_pages * PAGE_SIZE
    hq = IDX_HEADS * t

    def page_copy(src_ref, buf, sem, ch, slot, p, seq_idx=b):
        return pltpu.make_async_copy(src_ref.at[pt_ref[seq_idx, ch * ppc + p]],
                                     buf.at[slot, ..., pl.ds(p * PAGE_SIZE, PAGE_SIZE)], sem.at[slot])

    def start_chunk(src_ref, buf, sem, ch, slot, seq_idx=b):
        for p in range(ppc):
            page_copy(src_ref, buf, sem, ch, slot, p, seq_idx).start()

    def wait_chunk(src_ref, buf, sem, ch, slot):
        for p in range(ppc):
            page_copy(src_ref, buf, sem, ch, slot, p).wait()

    def prefetch(streams, nslot):
        for ch in range(min(nslot - 1, nch)):
            for stream in streams:
                start_chunk(*stream, ch, ch % nslot)

    def arrive(streams, nslot, ch):
        if ch + nslot - 1 < nch:
            for stream in streams:
                start_chunk(*stream, ch + nslot - 1, (ch + nslot - 1) % nslot)
        for stream in streams:
            wait_chunk(*stream, ch, ch % nslot)
        return ch % nslot

    kv_streams = [(ck_ref, kbuf, ksem), (cv_ref, vbuf, vsem)]

    qi_all = jnp.concatenate([qi_ref[0, :, h * LANE:h * LANE + IDX_DIM] for h in range(IDX_HEADS)],
                             axis=0).astype(BF)
    misc = misc_ref[0]
    w = misc[:, MISC_WI:MISC_WI + IDX_HEADS] * (IDX_HEADS ** -0.5)
    w_all = jnp.concatenate([w[:, h:h + 1] for h in range(IDX_HEADS)], axis=0)
    qpos = past + lax.broadcasted_iota(I32, (t, LANE), 0)

    nt_dims = (((1,), (1,)), ((), ()))

    def index_scores(d):
        r = jnp.maximum(d, 0.0) * w_all
        return jnp.sum(r.reshape(IDX_HEADS, t, r.shape[1]), axis=0)

    assert ibuf.shape[0] == nch

    def request_idx(seq_idx):
        for ch in range(nch):
            start_chunk(ckidx_ref, ibuf, isem, ch, ch, seq_idx)

    @pl.when(b == 0)
    def _():
        request_idx(b)

    for ch in range(nch):
        wait_chunk(ckidx_ref, ibuf, isem, ch, ch)
        d = jnp.dot(qi_all, ibuf[ch].astype(BF), preferred_element_type=F32)
        keys_ref[:, ch * kc:(ch + 1) * kc] = index_scores(d)

    @pl.when(b + 1 < pl.num_programs(0))
    def _():
        request_idx(b + 1)
    knew = _pad_rows(misc[:, :IDX_DIM], LANE).astype(BF)
    d_new = lax.dot_general(qi_all, knew, nt_dims, preferred_element_type=F32)
    new_pos = past + lax.broadcasted_iota(I32, (t, LANE), 1)
    keys_ref[:, past:past + LANE] = jnp.where(new_pos <= qpos, index_scores(d_new), -jnp.inf)

    prefetch(kv_streams, kbuf.shape[0])

    width = past + LANE
    col = lax.broadcasted_iota(I32, (t, width), 1)

    def count_where(indicator):
        return jnp.sum(_lane_fold(indicator(keys_ref[...])), axis=1, keepdims=True)

    thr_key = _radix_max(
        lambda cand: count_where(lambda s: jnp.where(s >= _key_to_f32(cand), 1.0, 0.0)),
        jnp.full((t, 1), INT_MIN, I32), 32, topk)
    thr = _key_to_f32(jnp.maximum(thr_key, KEY_LOWEST_FINITE))
    n_ge = count_where(lambda s: jnp.where(s >= thr, 1.0, 0.0))
    surplus = n_ge > topk

    @pl.when(jnp.max(jnp.where(surplus, 1, 0)) > 0)
    def _():
        need = topk - count_where(lambda s: jnp.where(s > thr, 1.0, 0.0))
        nbits = max(1, (width - 1).bit_length())
        cut = _tie_cut(lambda cand: count_where(
            lambda s: jnp.where(s == thr, jnp.where(col < cand, 1.0, 0.0), 0.0)), need, t, nbits)
        s = keys_ref[...]
        dropped = jnp.where(s == thr, jnp.where(col > cut, jnp.where(surplus, 1, 0), 0), 0)
        keys_ref[...] = jnp.where(dropped > 0, -jnp.inf, s)

    gt = GQA_GROUP * t
    qg = [jnp.concatenate([qa_ref[0, :, h * LANE + g * HEAD_DIM:h * LANE + (g + 1) * HEAD_DIM]
                           for h in range(g * GQA_GROUP, (g + 1) * GQA_GROUP)], axis=0).astype(BF)
          for g in range(N_KV_HEADS)]

    def attend(carry, scores, values, sel):
        m_old, l_old, acc_old = carry
        s = jnp.concatenate([scores(g) for g in range(N_KV_HEADS)], axis=0)
        n = s.shape[1]
        s = jnp.where(sel[None], s.reshape(N_HEADS, t, n), -jnp.inf).reshape(N_HEADS * t, n)
        m_new = jnp.maximum(m_old, jnp.max(s, axis=1, keepdims=True))
        alpha = jnp.exp2(m_old - m_new)
        p = jnp.exp2(s - m_new)
        l_new = alpha * l_old + jnp.sum(p, axis=1, keepdims=True)
        pb = p.astype(BF)
        pv = jnp.concatenate([values(g, pb[g * gt:(g + 1) * gt]) for g in range(N_KV_HEADS)], axis=0)
        return m_new, l_new, alpha * acc_old + pv

    carry = (jnp.full((hq, 1), NEG_BIG, F32), jnp.zeros((hq, 1), F32), jnp.zeros((hq, HEAD_DIM), F32))
    for ch in range(nch):
        slot = arrive(kv_streams, kbuf.shape[0], ch)
        sel = keys_ref[:, ch * kc:(ch + 1) * kc] >= thr
        carry = attend(
            carry,
            lambda g: jnp.dot(qg[g], kbuf[slot, g].astype(BF), preferred_element_type=F32),
            lambda g, pb: lax.dot_general(pb, vbuf[slot, g].astype(BF), nt_dims, preferred_element_type=F32),
            sel)
    sel_new = keys_ref[:, past:past + LANE] >= thr
    k_new = _pad_rows(ka_ref[0], LANE)
    v_new = _pad_rows(va_ref[0], LANE)
    carry = attend(
        carry,
        lambda g: lax.dot_general(qg[g], k_new[:, g * HEAD_DIM:(g + 1) * HEAD_DIM].astype(BF), nt_dims,
                                  preferred_element_type=F32),
        lambda g, pb: jnp.dot(pb, v_new[:, g * HEAD_DIM:(g + 1) * HEAD_DIM].astype(BF), preferred_element_type=F32),
        sel_new)
    _, l_fin, acc_fin = carry
    o = acc_fin / l_fin
    for h in range(N_HEADS):
        out_ref[0, :, h * HEAD_DIM:(h + 1) * HEAD_DIM] = o[h * t:(h + 1) * t].astype(out_ref.dtype)


def _dsa_sample(page_table, qi, misc, qa, ka, va, cache_kidx, cache_k, cache_v, ppc):
    batch, t, _ = misc.shape
    n_pages = page_table.shape[1]
    past = n_pages * PAGE_SIZE
    topk = min(TOPK_MAX, (past + t) // 4)
    kern = functools.partial(_dsa_sample_kernel, t=t, n_pages=n_pages, ppc=ppc, topk=topk)
    idx_slots = n_pages // ppc
    kv_slots = min(4, n_pages // ppc)
    row = lambda width: pl.BlockSpec((1, t, width), lambda b, pt: (b, 0, 0))
    hbm = pl.BlockSpec(memory_space=pl.ANY)
    grid_spec = pltpu.PrefetchScalarGridSpec(
        num_scalar_prefetch=1,
        grid=(batch,),
        in_specs=[row(IDX_HEADS * LANE), row(LANE), row(N_HEADS * LANE), row(LANE), row(LANE),
                  hbm, hbm, hbm],
        out_specs=row(N_HEADS * HEAD_DIM),
        scratch_shapes=[pltpu.VMEM((idx_slots, IDX_DIM, ppc * PAGE_SIZE), F32),
                        pltpu.VMEM((kv_slots, N_KV_HEADS, HEAD_DIM, ppc * PAGE_SIZE), F32),
                        pltpu.VMEM((kv_slots, N_KV_HEADS, HEAD_DIM, ppc * PAGE_SIZE), F32),
                        pltpu.SemaphoreType.DMA((idx_slots,)), pltpu.SemaphoreType.DMA((kv_slots,)),
                        pltpu.SemaphoreType.DMA((kv_slots,)),
                        pltpu.VMEM((t, past + LANE), F32)],
    )
    return pl.pallas_call(
        kern,
        grid_spec=grid_spec,
        out_shape=jax.ShapeDtypeStruct((batch, t, N_HEADS * HEAD_DIM), F32),
        compiler_params=pltpu.CompilerParams(dimension_semantics=("arbitrary",), vmem_limit_bytes=VMEM_LIMIT),
        name="dsa_sample",
    )(page_table, qi, misc, qa, ka, va, cache_kidx, cache_k, cache_v)


def _row_tile(m, cap):
    tm = min(m, cap)
    while m % tm:
        tm //= 2
    return tm


def _gla_and_post(x2d, attn, misc, gla_in, gates, weights, batch, seq, gla_s0, gla_chunk, gla_nchunks, gla_heads,
                  gla_out_dtype, tm):
    (_, wal, bal, gn, wao, wgo, wout, ln1g, ln1b, wf1, wf2, ln2g, ln2b) = weights
    b3 = lambda a: a.reshape(batch, seq, a.shape[-1])
    qb, kb, vb, gb = gla_in
    ob, s_new = _gla(b3(misc), wal, bal, b3(qb), b3(kb), b3(vb), b3(gb), gn, gla_s0,
                     gla_chunk, gla_nchunks, gla_heads, gla_out_dtype)
    ob = ob.reshape(batch * seq, GLA_DV_TOT).astype(BF)
    y = _post(x2d, attn, ob, *gates, wao, wgo, wout, ln1g, ln1b, wf1, wf2, ln2g, ln2b, tm)
    return y, s_new


def kernel(x_prompt, x_sample, cache_k, cache_v, cache_kidx, state_gla, page_table,
           w_in, w_alpha2, b_alpha, gla_norm_g, w_attn_o, w_gla_o, w_out,
           ln1_g, ln1_b, w_ff1, w_ff2, ln2_g, ln2_b):
    assert w_in.shape[0] == DEPTH == 1
    bp, tp, _ = x_prompt.shape
    bs, ts, _ = x_sample.shape
    n_pool = cache_k.shape[1]
    l = 0
    wal = jnp.zeros((LANE, GLA_DK_TOT), F32).at[MISC_AB:MISC_AB + GATE_RANK].set(w_alpha2[l])
    wal = wal.reshape(LANE, GLA_HEADS, GLA_DK).transpose(1, 0, 2).astype(BF)
    weights = (_pack_w_in(w_in[l]), wal, b_alpha[l].reshape(GLA_HEADS, 1, GLA_DK), gla_norm_g[l].reshape(1, GLA_DV),
               w_attn_o[l].astype(BF), w_gla_o[l].astype(BF), w_out[l].astype(BF),
               ln1_g[l].reshape(1, D_MODEL), ln1_b[l].reshape(1, D_MODEL),
               w_ff1[l].astype(BF), w_ff2[l].astype(BF), ln2_g[l].reshape(1, D_MODEL), ln2_b[l].reshape(1, D_MODEL))

    xp = x_prompt.reshape(bp * tp, D_MODEL)
    kc = min(512, tp)
    tq = min(256, tp)
    (qa, ka_t, va_t, kab_t, vab, qi, misc, ki_t, miscb_t, *rest) = _proj(xp, weights[0], kc, token_minor_seq=tp)
    attn = _dsa_prompt(qi, misc, miscb_t, qa, kab_t, vab, batch=bp, seq=tp, tq=tq, ts=min(128, tq), kc=kc)
    gchunk = min(GLA_CHUNK, tp)
    y_p, s_p = _gla_and_post(xp, attn, misc, rest[:4], rest[4:], weights, bp, tp, None, gchunk,
                             min(8, tp // gchunk), GLA_HEADS, BF, kc)

    def kv_t(a):
        return jnp.transpose(a.reshape(bp, N_KV_HEADS, HEAD_DIM, tp), (0, 3, 1, 2))[None]

    xs = x_sample.reshape(bs * ts, D_MODEL)
    tm_s = _row_tile(bs * ts, 512)
    (qa, ka, va, qi, misc, *rest) = _proj(xs, weights[0], tm_s)
    ckidx = jnp.transpose(cache_kidx[l], (0, 2, 1))
    ck = jnp.transpose(cache_k[l], (0, 2, 3, 1))
    cv = jnp.transpose(cache_v[l], (0, 2, 3, 1))
    f3 = lambda a: a.astype(F32).reshape(bs, ts, a.shape[-1])
    attn = _dsa_sample(page_table, f3(qi), f3(misc), f3(qa), f3(ka), f3(va), ckidx, ck, cv,
                       ppc=min(16, page_table.shape[1]))
    attn = attn.reshape(bs * ts, N_HEADS * HEAD_DIM).astype(BF)
    y_s, s_s = _gla_and_post(xs, attn, misc, rest[:4], rest[4:], weights, bs, ts, state_gla[l], ts, 1, GLA_HEADS,
                             F32, tm_s)

    def kv(a):
        return a.reshape(1, bs, ts, N_KV_HEADS, HEAD_DIM)

    return (y_p.reshape(bp, tp, D_MODEL), y_s.reshape(bs, ts, D_MODEL),
            kv_t(ka_t), kv_t(va_t), jnp.transpose(ki_t, (0, 2, 1))[None], s_p[None],
            kv(ka), kv(va), misc[:, :IDX_DIM].reshape(1, bs, ts, IDX_DIM), s_s[None])
```

```python
import functools

import jax
import jax.numpy as jnp
from jax import lax
from jax.experimental import pallas as pl
from jax.experimental.pallas import tpu as pltpu

D_MODEL = 1024
N_HEADS = 8
N_KV_HEADS = 2
HEAD_DIM = 64
GQA_GROUP = N_HEADS // N_KV_HEADS
IDX_HEADS = 8
IDX_DIM = 64
TOPK_MAX = 256
PAGE_SIZE = 128
GLA_HEADS = 4
GLA_DK_TOT = D_MODEL // 2
GLA_DV_TOT = D_MODEL
GLA_DK = GLA_DK_TOT // GLA_HEADS
GLA_DV = GLA_DV_TOT // GLA_HEADS
GATE_RANK = 16
GATE_TAU = 16.0
GLA_CHUNK = 64
D_FF = 4 * D_MODEL
NORM_EPS = 1e-5
DEPTH = 1
DN_ALPHA = (2 * DEPTH) ** 0.25

LANE = 128
SUBLANE = 8
VMEM_LIMIT = 56 * 1024 * 1024
BF = jnp.bfloat16
F32 = jnp.float32
I32 = jnp.int32
I16 = jnp.int16
INT_MIN = -2 ** 31
I16_MIN = -2 ** 15
LOG2_E = 1.4426950408889634
KEY_LOWEST_FINITE = -0x7F7FFFFF
NEG_BIG = -1e30

MISC_WI = IDX_DIM
MISC_AB = IDX_DIM + IDX_HEADS

_PROJ_GROUPS = (
    ('qa', N_HEADS * HEAD_DIM), ('ka', LANE), ('va', LANE), ('qi', IDX_HEADS * IDX_DIM), ('misc', LANE),
    ('qb', GLA_DK_TOT), ('kb', GLA_DK_TOT), ('vb', GLA_DV_TOT), ('gb', GLA_DV_TOT),
    ('ga', D_MODEL), ('gb2', D_MODEL),
)
_PROJ_OFF = {}
_o = 0
for _n, _w in _PROJ_GROUPS:
    _PROJ_OFF[_n] = (_o, _o + _w)
    _o += _w
PROJ_COLS = _o


def _const_spec(shape):
    nd = len(shape)
    return pl.BlockSpec(shape, lambda *_: (0,) * nd, pipeline_mode=pl.Buffered(1))


def _pack_w_in(w_in):
    off = 0
    parts = {}
    for name, width in (('q_a', 512), ('k_a', 128), ('v_a', 128), ('q_i', 512), ('k_i', 64), ('w_i', 8),
                        ('q_b', 512), ('k_b', 512), ('v_b', 1024), ('g_b', 1024), ('a_b', 16),
                        ('gate_a', 1024), ('gate_b', 1024)):
        parts[name] = w_in[:, off:off + width]
        off += width
    qa_heads = [parts['q_a'][:, h * HEAD_DIM:(h + 1) * HEAD_DIM] * (HEAD_DIM ** -0.5 * LOG2_E) for h in range(N_HEADS)]
    qa_cols = [qa_heads[g * GQA_GROUP + j] for j in range(GQA_GROUP) for g in range(N_KV_HEADS)]
    misc = jnp.concatenate([parts['k_i'], parts['w_i'], parts['a_b'],
                            jnp.zeros((D_MODEL, LANE - MISC_AB - GATE_RANK), F32)], axis=1)
    cols = qa_cols + [parts['k_a'], parts['v_a'], parts['q_i'] * (IDX_DIM ** -0.5), misc,
                      parts['q_b'], parts['k_b'], parts['v_b'], parts['g_b'], parts['gate_a'], parts['gate_b']]
    return jnp.concatenate(cols, axis=1).astype(BF)


def _proj_kernel(x_ref, w_ref, *out_refs, token_minor):
    xb = x_ref[...].astype(BF)

    def mm(name):
        a, b = _PROJ_OFF[name]
        return jnp.dot(xb, w_ref[:, a:b], preferred_element_type=F32)

    if token_minor:
        (qa_ref, kat_ref, vat_ref, kabt_ref, vab_ref, qi_ref, misc_ref, kit_ref, kibt_ref,
         qb_ref, kb_ref, vb_ref, gb_ref, ga_ref, gb2_ref, ka_tmp, va_tmp) = out_refs
    else:
        (qa_ref, ka_ref, va_ref, qi_ref, misc_ref, qb_ref, kb_ref, vb_ref, gb_ref, ga_ref, gb2_ref) = out_refs
    qa_ref[...] = mm('qa').astype(BF)
    qi_ref[...] = mm('qi').astype(BF)
    ka, va, misc = mm('ka'), mm('va'), mm('misc')
    misc_ref[...] = misc
    if token_minor:
        ka_tmp[...] = ka
        va_tmp[...] = va
        ka_t, va_t, misc_t = ka_tmp[...].T, va_tmp[...].T, misc_ref[...].T
        kat_ref[0] = ka_t
        vat_ref[0] = va_t
        ki_t = misc_t[:IDX_DIM]
        kit_ref[0] = ki_t
        kabt_ref[0] = ka_t.astype(BF)
        kibt_ref[0] = jnp.concatenate([ki_t, ki_t], axis=0).astype(BF)
        vab_ref[...] = va.astype(BF)
    else:
        ka_ref[...] = ka
        va_ref[...] = va
    qb_ref[...] = mm('qb')
    kb_ref[...] = mm('kb')
    vb_ref[...] = mm('vb')
    gb_ref[...] = mm('gb')
    ga_ref[...] = mm('ga')
    gb2_ref[...] = mm('gb2')


def _proj(x2d, w_packed, tm, token_minor_seq=None):
    m = x2d.shape[0]
    widths = dict(_PROJ_GROUPS)
    row = lambda n, dt: (pl.BlockSpec((tm, widths[n]), lambda i: (i, 0)), jax.ShapeDtypeStruct((m, widths[n]), dt))
    if token_minor_seq is None:
        outs = [row('qa', BF), row('ka', F32), row('va', F32), row('qi', BF), row('misc', F32)]
    else:
        nt = token_minor_seq // tm
        seq_t = lambda rows: (pl.BlockSpec((1, rows, tm), lambda i: (i // nt, 0, i % nt)),
                              jax.ShapeDtypeStruct((m // token_minor_seq, rows, token_minor_seq), F32))
        chunk_t = (pl.BlockSpec((1, LANE, tm), lambda i: (i, 0, 0)), jax.ShapeDtypeStruct((m // tm, LANE, tm), BF))
        outs = [row('qa', BF), seq_t(LANE), seq_t(LANE), chunk_t, row('va', BF), row('qi', BF), row('misc', F32),
                seq_t(IDX_DIM), chunk_t]
    outs += [row(n, F32) for n in ('qb', 'kb', 'vb', 'gb', 'ga', 'gb2')]
    return pl.pallas_call(
        functools.partial(_proj_kernel, token_minor=token_minor_seq is not None),
        grid=(m // tm,),
        in_specs=[pl.BlockSpec((tm, D_MODEL), lambda i: (i, 0)), _const_spec((D_MODEL, PROJ_COLS))],
        out_specs=[spec for spec, _ in outs],
        out_shape=[shape for _, shape in outs],
        scratch_shapes=[] if token_minor_seq is None else [pltpu.VMEM((tm, LANE), F32)] * 2,
        compiler_params=pltpu.CompilerParams(dimension_semantics=("parallel",), vmem_limit_bytes=VMEM_LIMIT),
        name="proj",
    )(x2d, w_packed)


def _key_to_f32(key):
    return pltpu.bitcast(jnp.where(key < 0, INT_MIN - key, key), F32)


def _tree_reduce(op, xs):
    xs = list(xs)
    while len(xs) > 1:
        xs = [op(xs[j], xs[j + 1]) if j + 1 < len(xs) else xs[j] for j in range(0, len(xs), 2)]
    return xs[0]


def _lane_fold(x, op=jnp.add):
    return _tree_reduce(op, [x[:, j * LANE:(j + 1) * LANE] for j in range(x.shape[1] // LANE)])


def _radix_max(count_ge, start, nbits, k):
    def body(it, ans):
        cand = ans + jnp.left_shift(jnp.int32(1), nbits - 1 - it)
        return jnp.where(count_ge(cand) >= k, cand, ans)
    return lax.fori_loop(0, nbits, body, start)


def _tie_cut(count_tie_below, need, rows, nbits):
    def body(it, m):
        cand = m | jnp.left_shift(jnp.int32(1), nbits - 1 - it)
        return jnp.where(count_tie_below(cand) < need, cand, m)
    return lax.fori_loop(0, nbits, body, jnp.zeros((rows, 1), I32))


def _dsa_prompt_kernel(qi_ref, misc_ref, kibt_ref, qa_ref, kabt_ref, vab_ref, out_ref,
                       sc_ref, hi_ref, lo_ref, s_ref, thr_ref, *, tq, ts, kc, topk, seq):
    i = pl.program_id(1)
    nkc = ((i + 1) * tq + kc - 1) // kc
    lane_pos = lax.broadcasted_iota(I32, (tq, kc), 1)
    lane_half = lax.broadcasted_iota(I32, (ts, LANE), 1) // HEAD_DIM

    def head_stack(ref, rows, block_and_half):
        parts = []
        for h in range(N_HEADS):
            blk, half = block_and_half(h)
            x = ref[rows, blk * LANE:(blk + 1) * LANE]
            parts.append(jnp.where(lane_half == half, x, jnp.zeros_like(x)))
        return jnp.concatenate(parts, axis=0)

    qi_stack = [head_stack(qi_ref, slice(r * ts, (r + 1) * ts), lambda h: (h // 2, h % 2)) for r in range(tq // ts)]

    def score_chunk(c, carry):
        kchunk_t = kibt_ref[c]
        for r in range(tq // ts):
            rows = slice(r * ts, (r + 1) * ts)
            w = misc_ref[rows, MISC_WI:MISC_WI + IDX_HEADS] * (IDX_HEADS ** -0.5)
            d = jnp.dot(qi_stack[r], kchunk_t, preferred_element_type=F32)
            acc = jnp.zeros((ts, kc), F32)
            for h in range(IDX_HEADS):
                acc = acc + jnp.maximum(d[h * ts:(h + 1) * ts], 0.0) * w[:, h:h + 1]
            tpos = i * tq + r * ts + lax.broadcasted_iota(I32, (ts, kc), 0)
            score = jnp.where(c * kc + lax.broadcasted_iota(I32, (ts, kc), 1) <= tpos, acc, -jnp.inf)
            sc_ref[c, rows, :] = score
            score_t = score.T
            lo_ref[c, :, rows] = score_t
            hi_ref[c, :, rows] = score_t.astype(BF)
        return carry
    lax.fori_loop(0, nkc, score_chunk, 0)

    def count_where(indicator):
        def body(c, cnt):
            return cnt + _lane_fold(indicator(c, sc_ref[c]))
        cnt = lax.fori_loop(0, nkc, body, jnp.zeros((tq, LANE), F32))
        return jnp.sum(cnt, axis=1, keepdims=True)

    def count_keys(ref, pack, zero, indicator):
        def body(c, cnt):
            ind = indicator(ref[c].reshape(kc // pack, pack, tq))
            return cnt + _tree_reduce(jnp.add, [ind[j] for j in range(kc // pack)])
        cnt = lax.fori_loop(0, nkc, body, jnp.zeros((pack, tq), zero.dtype)).astype(F32)
        cnt = _tree_reduce(jnp.add, [cnt[j * SUBLANE:(j + 1) * SUBLANE] for j in range(pack // SUBLANE)])
        for shift in (4, 2, 1):
            cnt = cnt + pltpu.roll(cnt, shift, axis=0)
        return cnt

    def bf16_rows(key16):
        bits = jnp.where(key16 < 0, I16_MIN - key16, key16)
        val = pltpu.bitcast(jnp.left_shift(bits, 16), F32)
        return jnp.concatenate([val, val], axis=0).astype(BF)[None]

    one16, zero16 = jnp.int16(1), jnp.int16(0)
    thr_hi = _radix_max(
        lambda cand: count_keys(hi_ref, 2 * SUBLANE, zero16, lambda v: jnp.where(v >= bf16_rows(cand), one16, zero16)),
        jnp.full((SUBLANE, tq), I16_MIN, I32), 16, float(topk))
    thr_key = _radix_max(
        lambda cand: count_keys(lo_ref, SUBLANE, jnp.float32(0),
                                lambda v: jnp.where(v >= _key_to_f32(cand)[None], 1.0, 0.0)),
        jnp.left_shift(thr_hi - 1, 16), 17, float(topk))
    thr_t = _key_to_f32(jnp.maximum(thr_key, KEY_LOWEST_FINITE))
    thr_rows = jnp.transpose(jnp.broadcast_to(thr_t[0:1], (LANE, tq)))
    thr = thr_rows[:, 0:1]
    n_ge = count_where(lambda c, s: jnp.where(s >= thr, 1.0, 0.0))
    surplus = jnp.where(n_ge > topk, 1, 0)

    @pl.when(jnp.max(surplus) > 0)
    def _():
        need = topk - count_where(lambda c, s: jnp.where(s > thr, 1.0, 0.0))
        nbits = max(1, (seq - 1).bit_length())
        cut = _tie_cut(lambda cand: count_where(
            lambda c, s: jnp.where(s == thr, jnp.where(c * kc + lane_pos < cand, 1.0, 0.0), 0.0)), need, tq, nbits)

        def drop(c, carry):
            s = sc_ref[c]
            dropped = jnp.where(s == thr, jnp.where(c * kc + lane_pos > cut, surplus, 0), 0)
            sc_ref[c] = jnp.where(dropped > 0, -jnp.inf, s)
            return carry
        lax.fori_loop(0, nkc, drop, 0)

    v_lanes = lax.broadcasted_iota(I32, (kc, LANE), 1) // HEAD_DIM
    thr_ref[...] = jnp.broadcast_to(thr, (tq, LANE))
    for r in range(tq // ts):
        rows = slice(r * ts, (r + 1) * ts)
        nkc_r = (i * tq + (r + 1) * ts + kc - 1) // kc
        thr_r = thr_ref[rows, 0:1]
        qa_stack = head_stack(qa_ref, rows, lambda h: (h % GQA_GROUP, h // GQA_GROUP))

        def score_pass(c, mrun):
            sel = sc_ref[c, rows, :] >= thr_r
            s = jnp.dot(qa_stack, kabt_ref[c], preferred_element_type=F32)
            s = jnp.where(sel[None], s.reshape(N_HEADS, ts, kc), -jnp.inf).reshape(N_HEADS * ts, kc)
            out = []
            for g in range(N_KV_HEADS):
                s_g = s[g * GQA_GROUP * ts:(g + 1) * GQA_GROUP * ts]
                s_ref[c, g] = s_g
                out.append(jnp.maximum(mrun[g], _lane_fold(s_g, jnp.maximum)))
            return tuple(out)
        mrun = lax.fori_loop(0, nkc_r, score_pass,
                             tuple(jnp.full((GQA_GROUP * ts, LANE), -jnp.inf, F32) for _ in range(N_KV_HEADS)))
        m = [jnp.max(mrun[g], axis=1, keepdims=True) for g in range(N_KV_HEADS)]

        def value_pass(c, acc):
            vchunk = vab_ref[pl.ds(pl.multiple_of(c * kc, kc), kc), :]
            out = []
            for g in range(N_KV_HEADS):
                vaug = jnp.where(v_lanes == g, vchunk, jnp.ones_like(vchunk))
                p = jnp.exp2(s_ref[c, g] - m[g]).astype(BF)
                out.append(acc[g] + jnp.dot(p, vaug, preferred_element_type=F32))
            return tuple(out)
        acc = lax.fori_loop(0, nkc_r, value_pass,
                            tuple(jnp.zeros((GQA_GROUP * ts, LANE), F32) for _ in range(N_KV_HEADS)))

        for g in range(N_KV_HEADS):
            ones_lane = (1 - g) * HEAD_DIM
            o = acc[g] / acc[g][:, ones_lane:ones_lane + 1]
            for j in range(GQA_GROUP):
                h = g * GQA_GROUP + j
                out_ref[rows, h * HEAD_DIM:(h + 1) * HEAD_DIM] = (
                    o[j * ts:(j + 1) * ts, g * HEAD_DIM:(g + 1) * HEAD_DIM].astype(out_ref.dtype))


def _dsa_prompt(qi, misc, miscb_t, qa, kab_t, vab, batch, seq, tq, ts, kc):
    topk = min(TOPK_MAX, seq // 4)
    nq = seq // tq
    kern = functools.partial(_dsa_prompt_kernel, tq=tq, ts=ts, kc=kc, topk=topk, seq=seq)
    qrow = lambda b, i: (b * nq + i, 0)
    brow = lambda b, i: (b, 0)
    bchunks = pl.BlockSpec((seq // kc, LANE, kc), lambda b, i: (b, 0, 0))
    return pl.pallas_call(
        kern,
        grid=(batch, nq),
        in_specs=[pl.BlockSpec((tq, IDX_HEADS * IDX_DIM), qrow), pl.BlockSpec((tq, LANE), qrow),
                  bchunks, pl.BlockSpec((tq, N_HEADS * HEAD_DIM), qrow),
                  bchunks, pl.BlockSpec((seq, LANE), brow)],
        out_specs=pl.BlockSpec((tq, N_HEADS * HEAD_DIM), qrow),
        out_shape=jax.ShapeDtypeStruct((batch * seq, N_HEADS * HEAD_DIM), BF),
        scratch_shapes=[pltpu.VMEM((seq // kc, tq, kc), F32),
                        pltpu.VMEM((seq // kc, kc, tq), BF),
                        pltpu.VMEM((seq // kc, kc, tq), F32),
                        pltpu.VMEM((seq // kc, N_KV_HEADS, GQA_GROUP * ts, kc), F32),
                        pltpu.VMEM((tq, LANE), F32)],
        compiler_params=pltpu.CompilerParams(dimension_semantics=("parallel", "arbitrary"),
                                             vmem_limit_bytes=VMEM_LIMIT),
        name="dsa_prompt",
    )(qi, misc, miscb_t, qa, kab_t, vab)


def _log_sigmoid(x):
    return jnp.minimum(x, 0.0) - jnp.log1p(jnp.exp(-jnp.abs(x)))


def _pad_rows(x, n):
    if x.shape[0] == n:
        return x
    return jnp.concatenate([x, jnp.zeros((n - x.shape[0], x.shape[1]), x.dtype)], axis=0)


def _gla_kernel(*refs, chunk, nchunks, heads, has_s0):
    if has_s0:
        (misc_ref, wal_ref, bal_ref, q_ref, k_ref, v_ref, g_ref, gn_ref, s0_ref, o_ref, sout_ref, s_ref) = refs
    else:
        (misc_ref, wal_ref, bal_ref, q_ref, k_ref, v_ref, g_ref, gn_ref, o_ref, sout_ref, s_ref) = refs
        s0_ref = None
    t = pl.program_id(2)
    rows = chunk * nchunks
    cpad = max(chunk, 2 * SUBLANE)
    rpad = max(rows, 2 * SUBLANE)

    @pl.when(t == 0)
    def _():
        s_ref[...] = jnp.zeros(s_ref.shape, F32) if s0_ref is None else s0_ref[0]

    misc_b = _pad_rows(misc_ref[0], rpad).astype(BF)
    pos = lax.broadcasted_iota(I32, (rows, GLA_DK), 0) % chunk
    cums = []
    for hh in range(heads):
        la = jnp.dot(misc_b, wal_ref[hh], preferred_element_type=F32)[:rows]
        cum = _log_sigmoid(la + bal_ref[hh]) / GATE_TAU
        d = 1
        while d < chunk:
            cum = cum + jnp.where(pos >= d, pltpu.roll(cum, d, axis=0), 0.0)
            d *= 2
        cums.append(cum)

    causal = (lax.broadcasted_iota(I32, (cpad, cpad), 0) >= lax.broadcasted_iota(I32, (cpad, cpad), 1))
    gn = gn_ref[...]
    for c in range(nchunks):
        r = slice(c * chunk, (c + 1) * chunk)
        for hh in range(heads):
            kcols = slice(hh * GLA_DK, (hh + 1) * GLA_DK)
            vcols = slice(hh * GLA_DV, (hh + 1) * GLA_DV)
            cum_c = cums[hh][r]
            last = cum_c[chunk - 1:chunk]
            q_c = q_ref[0, r, kcols] * (GLA_DK ** -0.5)
            k_c = k_ref[0, r, kcols]
            v_b = _pad_rows(v_ref[0, r, vcols], cpad).astype(BF)
            qt = _pad_rows(q_c * jnp.exp(cum_c), cpad).astype(BF)
            kt = _pad_rows(k_c * jnp.exp(-cum_c), cpad).astype(BF)
            kd = _pad_rows(k_c * jnp.exp(last - cum_c), cpad).astype(BF)
            att = lax.dot_general(qt, kt, (((1,), (1,)), ((), ())), preferred_element_type=F32)
            att = jnp.where(causal, att, 0.0).astype(BF)
            s_old = s_ref[hh]
            o = (jnp.dot(att, v_b, preferred_element_type=F32)
                 + jnp.dot(qt, s_old.astype(BF), preferred_element_type=F32))[:chunk]
            decay = jnp.transpose(jnp.broadcast_to(jnp.exp(last), (GLA_DK, GLA_DK)))
            upd = lax.dot_general(kd, v_b, (((0,), (0,)), ((), ())), preferred_element_type=F32)
            s_ref[hh] = jnp.concatenate([decay] * (GLA_DV // GLA_DK), axis=1) * s_old + upd
            o = o * lax.rsqrt(jnp.mean(o * o, axis=-1, keepdims=True) + NORM_EPS) * gn
            gate = g_ref[0, r, vcols]
            o_ref[0, r, vcols] = (o * (gate * jax.nn.sigmoid(gate))).astype(o_ref.dtype)

    @pl.when(t == pl.num_programs(2) - 1)
    def _():
        sout_ref[0] = s_ref[...]


def _gla(misc, wal, bal, qb, kb, vb, gb, gn, s0, chunk, nchunks, heads, out_dtype):
    batch, seq, _ = misc.shape
    rows = chunk * nchunks
    head = lambda b, h, t: (b, t, h)
    in_specs = [pl.BlockSpec((1, rows, LANE), lambda b, h, t: (b, t, 0)),
                pl.BlockSpec((heads, LANE, GLA_DK), lambda b, h, t: (h, 0, 0)),
                pl.BlockSpec((heads, 1, GLA_DK), lambda b, h, t: (h, 0, 0)),
                pl.BlockSpec((1, rows, heads * GLA_DK), head), pl.BlockSpec((1, rows, heads * GLA_DK), head),
                pl.BlockSpec((1, rows, heads * GLA_DV), head), pl.BlockSpec((1, rows, heads * GLA_DV), head),
                pl.BlockSpec((1, GLA_DV), lambda b, h, t: (0, 0))]
    args = [misc, wal, bal, qb, kb, vb, gb, gn]
    state_spec = pl.BlockSpec((1, heads, GLA_DK, GLA_DV), lambda b, h, t: (b, h, 0, 0))
    if s0 is not None:
        in_specs.append(state_spec)
        args.append(s0)
    kern = functools.partial(_gla_kernel, chunk=chunk, nchunks=nchunks, heads=heads, has_s0=s0 is not None)
    return pl.pallas_call(
        kern,
        grid=(batch, GLA_HEADS // heads, seq // rows),
        in_specs=in_specs,
        out_specs=[pl.BlockSpec((1, rows, heads * GLA_DV), head), state_spec],
        out_shape=[jax.ShapeDtypeStruct((batch, seq, GLA_DV_TOT), out_dtype),
                   jax.ShapeDtypeStruct((batch, GLA_HEADS, GLA_DK, GLA_DV), F32)],
        scratch_shapes=[pltpu.VMEM((heads, GLA_DK, GLA_DV), F32)],
        compiler_params=pltpu.CompilerParams(dimension_semantics=("parallel", "parallel", "arbitrary"),
                                             vmem_limit_bytes=VMEM_LIMIT),
        name="gla",
    )(*args)


def _layer_norm(x, g, b):
    xc = x - jnp.mean(x, axis=-1, keepdims=True)
    var = jnp.mean(xc * xc, axis=-1, keepdims=True)
    return xc * lax.rsqrt(var + NORM_EPS) * g + b


def _post_kernel(x_ref, attn_ref, ob_ref, ga_ref, gb_ref, wao_ref, wgo_ref, wout_ref, ln1g_ref, ln1b_ref,
                 wf1_ref, wf2_ref, ln2g_ref, ln2b_ref, out_ref, *, ff_chunk):
    branch_a = jnp.dot(attn_ref[...], wao_ref[...], preferred_element_type=F32)
    branch_b = jnp.dot(ob_ref[...], wgo_ref[...], preferred_element_type=F32)
    merged = jax.nn.sigmoid(ga_ref[...]) * branch_a + jax.nn.sigmoid(gb_ref[...]) * branch_b
    mix = jnp.dot(merged.astype(BF), wout_ref[...], preferred_element_type=F32)
    h = _layer_norm(DN_ALPHA * x_ref[...] + mix, ln1g_ref[...], ln1b_ref[...])
    hb = h.astype(BF)
    ff = jnp.zeros(h.shape, F32)
    for c in range(D_FF // ff_chunk):
        cols = slice(c * ff_chunk, (c + 1) * ff_chunk)
        a = jnp.maximum(jnp.dot(hb, wf1_ref[:, cols], preferred_element_type=F32), 0.0)
        ff = ff + jnp.dot((a * a).astype(BF), wf2_ref[cols, :], preferred_element_type=F32)
    out_ref[...] = _layer_norm(DN_ALPHA * h + ff, ln2g_ref[...], ln2b_ref[...])


def _post(x2d, attn, ob, ga, gb, wao, wgo, wout, ln1g, ln1b, wf1, wf2, ln2g, ln2b, tm):
    m = x2d.shape[0]
    row = lambda width: pl.BlockSpec((tm, width), lambda i: (i, 0))
    return pl.pallas_call(
        functools.partial(_post_kernel, ff_chunk=1024),
        grid=(m // tm,),
        in_specs=[row(D_MODEL), row(N_HEADS * HEAD_DIM), row(GLA_DV_TOT), row(D_MODEL), row(D_MODEL),
                  _const_spec(wao.shape), _const_spec(wgo.shape), _const_spec(wout.shape),
                  _const_spec(ln1g.shape), _const_spec(ln1b.shape), _const_spec(wf1.shape), _const_spec(wf2.shape),
                  _const_spec(ln2g.shape), _const_spec(ln2b.shape)],
        out_specs=row(D_MODEL),
        out_shape=jax.ShapeDtypeStruct((m, D_MODEL), F32),
        compiler_params=pltpu.CompilerParams(dimension_semantics=("parallel",), vmem_limit_bytes=VMEM_LIMIT),
        name="post",
    )(x2d, attn, ob, ga, gb, wao, wgo, wout, ln1g, ln1b, wf1, wf2, ln2g, ln2b)


def _dsa_sample_kernel(pt_ref, qi_ref, misc_ref, qa_ref, ka_ref, va_ref,
                       ckidx_ref, ck_ref, cv_ref, out_ref,
                       ibuf, kbuf, vbuf, isem, ksem, vsem, keys_ref, *, t, n_pages, ppc, topk):
    b = pl.program_id(0)
    nch = n_pages // ppc
    kc = ppc * PAGE_SIZE
    past = n_pages * PAGE_SIZE
    hq = IDX_HEADS * t

    def page_copy(src_ref, buf, sem, ch, slot, p, seq_idx=b):
        return pltpu.make_async_copy(src_ref.at[pt_ref[seq_idx, ch * ppc + p]],
                                     buf.at[slot, ..., pl.ds(p * PAGE_SIZE, PAGE_SIZE)], sem.at[slot])

    def start_chunk(src_ref, buf, sem, ch, slot, seq_idx=b):
        for p in range(ppc):
            page_copy(src_ref, buf, sem, ch, slot, p, seq_idx).start()

    def wait_chunk(src_ref, buf, sem, ch, slot):
        for p in range(ppc):
            page_copy(src_ref, buf, sem, ch, slot, p).wait()

    def prefetch(streams, nslot):
        for ch in range(min(nslot - 1, nch)):
            for stream in streams:
                start_chunk(*stream, ch, ch % nslot)

    def arrive(streams, nslot, ch):
        if ch + nslot - 1 < nch:
            for stream in streams:
                start_chunk(*stream, ch + nslot - 1, (ch + nslot - 1) % nslot)
        for stream in streams:
            wait_chunk(*stream, ch, ch % nslot)
        return ch % nslot

    kv_streams = [(ck_ref, kbuf, ksem), (cv_ref, vbuf, vsem)]

    qi_all = jnp.concatenate([qi_ref[0, :, h * IDX_DIM:(h + 1) * IDX_DIM] for h in range(IDX_HEADS)],
                             axis=0).astype(BF)
    misc = misc_ref[0]
    w = misc[:, MISC_WI:MISC_WI + IDX_HEADS] * (IDX_HEADS ** -0.5)
    w_all = jnp.concatenate([w[:, h:h + 1] for h in range(IDX_HEADS)], axis=0)
    qpos = past + lax.broadcasted_iota(I32, (t, LANE), 0)

    nt_dims = (((1,), (1,)), ((), ()))

    def index_scores(d):
        r = jnp.maximum(d, 0.0) * w_all
        return jnp.sum(r.reshape(IDX_HEADS, t, r.shape[1]), axis=0)

    assert ibuf.shape[0] == nch

    def request_idx(seq_idx):
        for ch in range(nch):
            start_chunk(ckidx_ref, ibuf, isem, ch, ch, seq_idx)

    @pl.when(b == 0)
    def _():
        request_idx(b)

    for ch in range(nch):
        wait_chunk(ckidx_ref, ibuf, isem, ch, ch)
        d = jnp.dot(qi_all, ibuf[ch].astype(BF), preferred_element_type=F32)
        keys_ref[:, ch * kc:(ch + 1) * kc] = index_scores(d)

    @pl.when(b + 1 < pl.num_programs(0))
    def _():
        request_idx(b + 1)
    knew = _pad_rows(misc[:, :IDX_DIM], LANE).astype(BF)
    d_new = lax.dot_general(qi_all, knew, nt_dims, preferred_element_type=F32)
    new_pos = past + lax.broadcasted_iota(I32, (t, LANE), 1)
    keys_ref[:, past:past + LANE] = jnp.where(new_pos <= qpos, index_scores(d_new), -jnp.inf)

    prefetch(kv_streams, kbuf.shape[0])

    width = past + LANE
    col = lax.broadcasted_iota(I32, (t, width), 1)

    def count_where(indicator):
        return jnp.sum(_lane_fold(indicator(keys_ref[...])), axis=1, keepdims=True)

    thr_key = _radix_max(
        lambda cand: count_where(lambda s: jnp.where(s >= _key_to_f32(cand), 1.0, 0.0)),
        jnp.full((t, 1), INT_MIN, I32), 32, topk)
    thr = _key_to_f32(jnp.maximum(thr_key, KEY_LOWEST_FINITE))
    n_ge = count_where(lambda s: jnp.where(s >= thr, 1.0, 0.0))
    surplus = n_ge > topk

    @pl.when(jnp.max(jnp.where(surplus, 1, 0)) > 0)
    def _():
        need = topk - count_where(lambda s: jnp.where(s > thr, 1.0, 0.0))
        nbits = max(1, (width - 1).bit_length())
        cut = _tie_cut(lambda cand: count_where(
            lambda s: jnp.where(s == thr, jnp.where(col < cand, 1.0, 0.0), 0.0)), need, t, nbits)
        s = keys_ref[...]
        dropped = jnp.where(s == thr, jnp.where(col > cut, jnp.where(surplus, 1, 0), 0), 0)
        keys_ref[...] = jnp.where(dropped > 0, -jnp.inf, s)

    gt = GQA_GROUP * t
    qg = [jnp.concatenate([qa_ref[0, :, j * LANE + g * HEAD_DIM:j * LANE + (g + 1) * HEAD_DIM]
                           for j in range(GQA_GROUP)], axis=0).astype(BF)
          for g in range(N_KV_HEADS)]

    def attend(carry, scores, values, sel):
        m_old, l_old, acc_old = carry
        s = jnp.concatenate([scores(g) for g in range(N_KV_HEADS)], axis=0)
        n = s.shape[1]
        s = jnp.where(sel[None], s.reshape(N_HEADS, t, n), -jnp.inf).reshape(N_HEADS * t, n)
        m_new = jnp.maximum(m_old, jnp.max(s, axis=1, keepdims=True))
        alpha = jnp.exp2(m_old - m_new)
        p = jnp.exp2(s - m_new)
        l_new = alpha * l_old + jnp.sum(p, axis=1, keepdims=True)
        pb = p.astype(BF)
        pv = jnp.concatenate([values(g, pb[g * gt:(g + 1) * gt]) for g in range(N_KV_HEADS)], axis=0)
        return m_new, l_new, alpha * acc_old + pv

    carry = (jnp.full((hq, 1), NEG_BIG, F32), jnp.zeros((hq, 1), F32), jnp.zeros((hq, HEAD_DIM), F32))
    for ch in range(nch):
        slot = arrive(kv_streams, kbuf.shape[0], ch)
        sel = keys_ref[:, ch * kc:(ch + 1) * kc] >= thr
        carry = attend(
            carry,
            lambda g: jnp.dot(qg[g], kbuf[slot, g].astype(BF), preferred_element_type=F32),
            lambda g, pb: lax.dot_general(pb, vbuf[slot, g].astype(BF), nt_dims, preferred_element_type=F32),
            sel)
    sel_new = keys_ref[:, past:past + LANE] >= thr
    k_new = _pad_rows(ka_ref[0], LANE)
    v_new = _pad_rows(va_ref[0], LANE)
    carry = attend(
        carry,
        lambda g: lax.dot_general(qg[g], k_new[:, g * HEAD_DIM:(g + 1) * HEAD_DIM].astype(BF), nt_dims,
                                  preferred_element_type=F32),
        lambda g, pb: jnp.dot(pb, v_new[:, g * HEAD_DIM:(g + 1) * HEAD_DIM].astype(BF), preferred_element_type=F32),
        sel_new)
    _, l_fin, acc_fin = carry
    o = acc_fin / l_fin
    for h in range(N_HEADS):
        out_ref[0, :, h * HEAD_DIM:(h + 1) * HEAD_DIM] = o[h * t:(h + 1) * t].astype(out_ref.dtype)


def _dsa_sample(page_table, qi, misc, qa, ka, va, cache_kidx, cache_k, cache_v, ppc):
    batch, t, _ = misc.shape
    n_pages = page_table.shape[1]
    past = n_pages * PAGE_SIZE
    topk = min(TOPK_MAX, (past + t) // 4)
    kern = functools.partial(_dsa_sample_kernel, t=t, n_pages=n_pages, ppc=ppc, topk=topk)
    idx_slots = n_pages // ppc
    kv_slots = min(4, n_pages // ppc)
    row = lambda width: pl.BlockSpec((1, t, width), lambda b, pt: (b, 0, 0))
    hbm = pl.BlockSpec(memory_space=pl.ANY)
    grid_spec = pltpu.PrefetchScalarGridSpec(
        num_scalar_prefetch=1,
        grid=(batch,),
        in_specs=[row(IDX_HEADS * IDX_DIM), row(LANE), row(N_HEADS * HEAD_DIM), row(LANE), row(LANE),
                  hbm, hbm, hbm],
        out_specs=row(N_HEADS * HEAD_DIM),
        scratch_shapes=[pltpu.VMEM((idx_slots, IDX_DIM, ppc * PAGE_SIZE), F32),
                        pltpu.VMEM((kv_slots, N_KV_HEADS, HEAD_DIM, ppc * PAGE_SIZE), F32),
                        pltpu.VMEM((kv_slots, N_KV_HEADS, HEAD_DIM, ppc * PAGE_SIZE), F32),
                        pltpu.SemaphoreType.DMA((idx_slots,)), pltpu.SemaphoreType.DMA((kv_slots,)),
                        pltpu.SemaphoreType.DMA((kv_slots,)),
                        pltpu.VMEM((t, past + LANE), F32)],
    )
    return pl.pallas_call(
        kern,
        grid_spec=grid_spec,
        out_shape=jax.ShapeDtypeStruct((batch, t, N_HEADS * HEAD_DIM), F32),
        compiler_params=pltpu.CompilerParams(dimension_semantics=("arbitrary",), vmem_limit_bytes=VMEM_LIMIT),
        name="dsa_sample",
    )(page_table, qi, misc, qa, ka, va, cache_kidx, cache_k, cache_v)


def _row_tile(m, cap):
    tm = min(m, cap)
    while m % tm:
        tm //= 2
    return tm


def _gla_and_post(x2d, attn, misc, gla_in, gates, weights, batch, seq, gla_s0, gla_chunk, gla_nchunks, gla_heads,
                  gla_out_dtype, tm):
    (_, wal, bal, gn, wao, wgo, wout, ln1g, ln1b, wf1, wf2, ln2g, ln2b) = weights
    b3 = lambda a: a.reshape(batch, seq, a.shape[-1])
    qb, kb, vb, gb = gla_in
    ob, s_new = _gla(b3(misc), wal, bal, b3(qb), b3(kb), b3(vb), b3(gb), gn, gla_s0,
                     gla_chunk, gla_nchunks, gla_heads, gla_out_dtype)
    ob = ob.reshape(batch * seq, GLA_DV_TOT).astype(BF)
    y = _post(x2d, attn, ob, *gates, wao, wgo, wout, ln1g, ln1b, wf1, wf2, ln2g, ln2b, tm)
    return y, s_new


def kernel(x_prompt, x_sample, cache_k, cache_v, cache_kidx, state_gla, page_table,
           w_in, w_alpha2, b_alpha, gla_norm_g, w_attn_o, w_gla_o, w_out,
           ln1_g, ln1_b, w_ff1, w_ff2, ln2_g, ln2_b):
    assert w_in.shape[0] == DEPTH == 1
    bp, tp, _ = x_prompt.shape
    bs, ts, _ = x_sample.shape
    n_pool = cache_k.shape[1]
    l = 0
    wal = jnp.zeros((LANE, GLA_DK_TOT), F32).at[MISC_AB:MISC_AB + GATE_RANK].set(w_alpha2[l])
    wal = wal.reshape(LANE, GLA_HEADS, GLA_DK).transpose(1, 0, 2).astype(BF)
    weights = (_pack_w_in(w_in[l]), wal, b_alpha[l].reshape(GLA_HEADS, 1, GLA_DK), gla_norm_g[l].reshape(1, GLA_DV),
               w_attn_o[l].astype(BF), w_gla_o[l].astype(BF), w_out[l].astype(BF),
               ln1_g[l].reshape(1, D_MODEL), ln1_b[l].reshape(1, D_MODEL),
               w_ff1[l].astype(BF), w_ff2[l].astype(BF), ln2_g[l].reshape(1, D_MODEL), ln2_b[l].reshape(1, D_MODEL))

    xp = x_prompt.reshape(bp * tp, D_MODEL)
    kc = min(512, tp)
    tq = min(256, tp)
    (qa, ka_t, va_t, kab_t, vab, qi, misc, ki_t, miscb_t, *rest) = _proj(xp, weights[0], kc, token_minor_seq=tp)
    attn = _dsa_prompt(qi, misc, miscb_t, qa, kab_t, vab, batch=bp, seq=tp, tq=tq, ts=min(128, tq), kc=kc)
    gchunk = min(GLA_CHUNK, tp)
    y_p, s_p = _gla_and_post(xp, attn, misc, rest[:4], rest[4:], weights, bp, tp, None, gchunk,
                             min(8, tp // gchunk), GLA_HEADS, BF, kc)

    def kv_t(a):
        return jnp.transpose(a.reshape(bp, N_KV_HEADS, HEAD_DIM, tp), (0, 3, 1, 2))[None]

    xs = x_sample.reshape(bs * ts, D_MODEL)
    tm_s = _row_tile(bs * ts, 512)
    (qa, ka, va, qi, misc, *rest) = _proj(xs, weights[0], tm_s)
    ckidx = jnp.transpose(cache_kidx[l], (0, 2, 1))
    ck = jnp.transpose(cache_k[l], (0, 2, 3, 1))
    cv = jnp.transpose(cache_v[l], (0, 2, 3, 1))
    f3 = lambda a: a.astype(F32).reshape(bs, ts, a.shape[-1])
    attn = _dsa_sample(page_table, f3(qi), f3(misc), f3(qa), f3(ka), f3(va), ckidx, ck, cv,
                       ppc=min(16, page_table.shape[1]))
    attn = attn.reshape(bs * ts, N_HEADS * HEAD_DIM).astype(BF)
    y_s, s_s = _gla_and_post(xs, attn, misc, rest[:4], rest[4:], weights, bs, ts, state_gla[l], ts, 1, GLA_HEADS,
                             F32, tm_s)

    def kv(a):
        return a.reshape(1, bs, ts, N_KV_HEADS, HEAD_DIM)

    return (y_p.reshape(bp, tp, D_MODEL), y_s.reshape(bs, ts, D_MODEL),
            kv_t(ka_t), kv_t(va_t), jnp.transpose(ki_t, (0, 2, 1))[None], s_p[None],
            kv(ka), kv(va), misc[:, :IDX_DIM].reshape(1, bs, ts, IDX_DIM), s_s[None])
```

```python
import functools

import jax
import jax.numpy as jnp
from jax import lax
from jax.experimental import pallas as pl
from jax.experimental.pallas import tpu as pltpu

D_MODEL = 1024
N_HEADS = 8
N_KV_HEADS = 2
HEAD_DIM = 64
GQA_GROUP = N_HEADS // N_KV_HEADS
IDX_HEADS = 8
IDX_DIM = 64
TOPK_MAX = 256
PAGE_SIZE = 128
GLA_HEADS = 4
GLA_DK_TOT = D_MODEL // 2
GLA_DV_TOT = D_MODEL
GLA_DK = GLA_DK_TOT // GLA_HEADS
GLA_DV = GLA_DV_TOT // GLA_HEADS
GATE_RANK = 16
GATE_TAU = 16.0
GLA_CHUNK = 64
D_FF = 4 * D_MODEL
NORM_EPS = 1e-5
DEPTH = 1
DN_ALPHA = (2 * DEPTH) ** 0.25

LANE = 128
SUBLANE = 8
VMEM_LIMIT = 56 * 1024 * 1024
BF = jnp.bfloat16
F32 = jnp.float32
I32 = jnp.int32
I16 = jnp.int16
INT_MIN = -2 ** 31
I16_MIN = -2 ** 15
LOG2_E = 1.4426950408889634
KEY_LOWEST_FINITE = -0x7F7FFFFF
NEG_BIG = -1e30

MISC_WI = IDX_DIM
MISC_AB = IDX_DIM + IDX_HEADS

_PROJ_GROUPS = (
    ('qa', N_HEADS * HEAD_DIM), ('ka', LANE), ('va', LANE), ('qi', IDX_HEADS * IDX_DIM), ('misc', LANE),
    ('qb', GLA_DK_TOT), ('kb', GLA_DK_TOT), ('vb', GLA_DV_TOT), ('gb', GLA_DV_TOT),
    ('ga', D_MODEL), ('gb2', D_MODEL),
)
_PROJ_OFF = {}
_o = 0
for _n, _w in _PROJ_GROUPS:
    _PROJ_OFF[_n] = (_o, _o + _w)
    _o += _w
PROJ_COLS = _o


def _const_spec(shape):
    nd = len(shape)
    return pl.BlockSpec(shape, lambda *_: (0,) * nd, pipeline_mode=pl.Buffered(1))


def _pack_w_in(w_in):
    off = 0
    parts = {}
    for name, width in (('q_a', 512), ('k_a', 128), ('v_a', 128), ('q_i', 512), ('k_i', 64), ('w_i', 8),
                        ('q_b', 512), ('k_b', 512), ('v_b', 1024), ('g_b', 1024), ('a_b', 16),
                        ('gate_a', 1024), ('gate_b', 1024)):
        parts[name] = w_in[:, off:off + width]
        off += width
    qa_heads = [parts['q_a'][:, h * HEAD_DIM:(h + 1) * HEAD_DIM] * (HEAD_DIM ** -0.5 * LOG2_E) for h in range(N_HEADS)]
    qa_cols = [qa_heads[g * GQA_GROUP + j] for j in range(GQA_GROUP) for g in range(N_KV_HEADS)]
    misc = jnp.concatenate([parts['k_i'], parts['w_i'], parts['a_b'],
                            jnp.zeros((D_MODEL, LANE - MISC_AB - GATE_RANK), F32)], axis=1)
    cols = qa_cols + [parts['k_a'], parts['v_a'], parts['q_i'] * (IDX_DIM ** -0.5), misc,
                      parts['q_b'], parts['k_b'], parts['v_b'], parts['g_b'], parts['gate_a'], parts['gate_b']]
    return jnp.concatenate(cols, axis=1).astype(BF)


def _proj_kernel(x_ref, w_ref, *out_refs, token_minor):
    xb = x_ref[...].astype(BF)

    def mm(name):
        a, b = _PROJ_OFF[name]
        return jnp.dot(xb, w_ref[:, a:b], preferred_element_type=F32)

    if token_minor:
        (qa_ref, kat_ref, vat_ref, kabt_ref, vab_ref, qi_ref, misc_ref, kit_ref, kibt_ref,
         qb_ref, kb_ref, vb_ref, gb_ref, ga_ref, gb2_ref, ka_tmp, va_tmp) = out_refs
    else:
        (qa_ref, ka_ref, va_ref, qi_ref, misc_ref, qb_ref, kb_ref, vb_ref, gb_ref, ga_ref, gb2_ref) = out_refs
    qa_ref[...] = mm('qa').astype(BF)
    qi_ref[...] = mm('qi').astype(BF)
    ka, va, misc = mm('ka'), mm('va'), mm('misc')
    misc_ref[...] = misc
    if token_minor:
        ka_tmp[...] = ka
        va_tmp[...] = va
        ka_t, va_t, misc_t = ka_tmp[...].T, va_tmp[...].T, misc_ref[...].T
        kat_ref[0] = ka_t
        vat_ref[0] = va_t
        ki_t = misc_t[:IDX_DIM]
        kit_ref[0] = ki_t
        kabt_ref[0] = ka_t.astype(BF)
        kibt_ref[0] = jnp.concatenate([ki_t, ki_t], axis=0).astype(BF)
        vab_ref[...] = va.astype(BF)
    else:
        ka_ref[...] = ka
        va_ref[...] = va
    qb_ref[...] = mm('qb')
    kb_ref[...] = mm('kb')
    vb_ref[...] = mm('vb')
    gb_ref[...] = mm('gb')
    ga_ref[...] = mm('ga')
    gb2_ref[...] = mm('gb2')


def _proj(x2d, w_packed, tm, token_minor_seq=None):
    m = x2d.shape[0]
    widths = dict(_PROJ_GROUPS)
    row = lambda n, dt: (pl.BlockSpec((tm, widths[n]), lambda i: (i, 0)), jax.ShapeDtypeStruct((m, widths[n]), dt))
    if token_minor_seq is None:
        outs = [row('qa', BF), row('ka', F32), row('va', F32), row('qi', BF), row('misc', F32)]
    else:
        nt = token_minor_seq // tm
        seq_t = lambda rows: (pl.BlockSpec((1, rows, tm), lambda i: (i // nt, 0, i % nt)),
                              jax.ShapeDtypeStruct((m // token_minor_seq, rows, token_minor_seq), F32))
        chunk_t = (pl.BlockSpec((1, LANE, tm), lambda i: (i, 0, 0)), jax.ShapeDtypeStruct((m // tm, LANE, tm), BF))
        outs = [row('qa', BF), seq_t(LANE), seq_t(LANE), chunk_t, row('va', BF), row('qi', BF), row('misc', F32),
                seq_t(IDX_DIM), chunk_t]
    outs += [row(n, F32) for n in ('qb', 'kb', 'vb', 'gb', 'ga', 'gb2')]
    return pl.pallas_call(
        functools.partial(_proj_kernel, token_minor=token_minor_seq is not None),
        grid=(m // tm,),
        in_specs=[pl.BlockSpec((tm, D_MODEL), lambda i: (i, 0)), _const_spec((D_MODEL, PROJ_COLS))],
        out_specs=[spec for spec, _ in outs],
        out_shape=[shape for _, shape in outs],
        scratch_shapes=[] if token_minor_seq is None else [pltpu.VMEM((tm, LANE), F32)] * 2,
        compiler_params=pltpu.CompilerParams(dimension_semantics=("parallel",), vmem_limit_bytes=VMEM_LIMIT),
        name="proj",
    )(x2d, w_packed)


def _key_to_f32(key):
    return pltpu.bitcast(jnp.where(key < 0, INT_MIN - key, key), F32)


def _tree_reduce(op, xs):
    xs = list(xs)
    while len(xs) > 1:
        xs = [op(xs[j], xs[j + 1]) if j + 1 < len(xs) else xs[j] for j in range(0, len(xs), 2)]
    return xs[0]


def _lane_fold(x, op=jnp.add):
    return _tree_reduce(op, [x[:, j * LANE:(j + 1) * LANE] for j in range(x.shape[1] // LANE)])


def _radix_max(count_ge, start, nbits, k):
    def body(it, ans):
        cand = ans + jnp.left_shift(jnp.int32(1), nbits - 1 - it)
        return jnp.where(count_ge(cand) >= k, cand, ans)
    return lax.fori_loop(0, nbits, body, start)


def _tie_cut(count_tie_below, need, rows, nbits):
    def body(it, m):
        cand = m | jnp.left_shift(jnp.int32(1), nbits - 1 - it)
        return jnp.where(count_tie_below(cand) < need, cand, m)
    return lax.fori_loop(0, nbits, body, jnp.zeros((rows, 1), I32))


def _dsa_prompt_kernel(qi_ref, misc_ref, kibt_ref, qa_ref, kabt_ref, vab_ref, out_ref,
                       sc_ref, hi_ref, lo_ref, s_ref, thr_ref, *, tq, ts, kc, topk, seq):
    i = pl.program_id(1)
    nkc = ((i + 1) * tq + kc - 1) // kc
    lane_pos = lax.broadcasted_iota(I32, (tq, kc), 1)
    lane_half = lax.broadcasted_iota(I32, (ts, LANE), 1) // HEAD_DIM

    def head_stack(ref, rows, block_and_half):
        parts = []
        for h in range(N_HEADS):
            blk, half = block_and_half(h)
            x = ref[rows, blk * LANE:(blk + 1) * LANE]
            parts.append(jnp.where(lane_half == half, x, jnp.zeros_like(x)))
        return jnp.concatenate(parts, axis=0)

    qi_stack = [head_stack(qi_ref, slice(r * ts, (r + 1) * ts), lambda h: (h // 2, h % 2)) for r in range(tq // ts)]

    def score_chunk(c, carry):
        kchunk_t = kibt_ref[c]
        for r in range(tq // ts):
            rows = slice(r * ts, (r + 1) * ts)
            w = misc_ref[rows, MISC_WI:MISC_WI + IDX_HEADS] * (IDX_HEADS ** -0.5)
            d = jnp.dot(qi_stack[r], kchunk_t, preferred_element_type=F32)
            acc = jnp.zeros((ts, kc), F32)
            for h in range(IDX_HEADS):
                acc = acc + jnp.maximum(d[h * ts:(h + 1) * ts], 0.0) * w[:, h:h + 1]
            tpos = i * tq + r * ts + lax.broadcasted_iota(I32, (ts, kc), 0)
            score = jnp.where(c * kc + lax.broadcasted_iota(I32, (ts, kc), 1) <= tpos, acc, -jnp.inf)
            sc_ref[c, rows, :] = score
            score_t = score.T
            lo_ref[c, :, rows] = score_t
            hi_ref[c, :, rows] = score_t.astype(BF)
        return carry
    lax.fori_loop(0, nkc, score_chunk, 0)

    def count_where(indicator):
        def body(c, cnt):
            return cnt + _lane_fold(indicator(c, sc_ref[c]))
        cnt = lax.fori_loop(0, nkc, body, jnp.zeros((tq, LANE), F32))
        return jnp.sum(cnt, axis=1, keepdims=True)

    def count_keys(ref, pack, zero, indicator):
        def body(c, cnt):
            ind = indicator(ref[c].reshape(kc // pack, pack, tq))
            return cnt + _tree_reduce(jnp.add, [ind[j] for j in range(kc // pack)])
        cnt = lax.fori_loop(0, nkc, body, jnp.zeros((pack, tq), zero.dtype)).astype(F32)
        cnt = _tree_reduce(jnp.add, [cnt[j * SUBLANE:(j + 1) * SUBLANE] for j in range(pack // SUBLANE)])
        for shift in (4, 2, 1):
            cnt = cnt + pltpu.roll(cnt, shift, axis=0)
        return cnt

    def bf16_rows(key16):
        bits = jnp.where(key16 < 0, I16_MIN - key16, key16)
        val = pltpu.bitcast(jnp.left_shift(bits, 16), F32)
        return jnp.concatenate([val, val], axis=0).astype(BF)[None]

    one16, zero16 = jnp.int16(1), jnp.int16(0)
    thr_hi = _radix_max(
        lambda cand: count_keys(hi_ref, 2 * SUBLANE, zero16, lambda v: jnp.where(v >= bf16_rows(cand), one16, zero16)),
        jnp.full((SUBLANE, tq), I16_MIN, I32), 16, float(topk))
    thr_key = _radix_max(
        lambda cand: count_keys(lo_ref, SUBLANE, jnp.float32(0),
                                lambda v: jnp.where(v >= _key_to_f32(cand)[None], 1.0, 0.0)),
        jnp.left_shift(thr_hi - 1, 16), 17, float(topk))
    thr_t = _key_to_f32(jnp.maximum(thr_key, KEY_LOWEST_FINITE))
    thr_rows = jnp.transpose(jnp.broadcast_to(thr_t[0:1], (LANE, tq)))
    thr = thr_rows[:, 0:1]
    n_ge = count_where(lambda c, s: jnp.where(s >= thr, 1.0, 0.0))
    surplus = jnp.where(n_ge > topk, 1, 0)

    @pl.when(jnp.max(surplus) > 0)
    def _():
        need = topk - count_where(lambda c, s: jnp.where(s > thr, 1.0, 0.0))
        nbits = max(1, (seq - 1).bit_length())
        cut = _tie_cut(lambda cand: count_where(
            lambda c, s: jnp.where(s == thr, jnp.where(c * kc + lane_pos < cand, 1.0, 0.0), 0.0)), need, tq, nbits)

        def drop(c, carry):
            s = sc_ref[c]
            dropped = jnp.where(s == thr, jnp.where(c * kc + lane_pos > cut, surplus, 0), 0)
            sc_ref[c] = jnp.where(dropped > 0, -jnp.inf, s)
            return carry
        lax.fori_loop(0, nkc, drop, 0)

    v_lanes = lax.broadcasted_iota(I32, (kc, LANE), 1) // HEAD_DIM
    thr_ref[...] = jnp.broadcast_to(thr, (tq, LANE))
    for r in range(tq // ts):
        rows = slice(r * ts, (r + 1) * ts)
        nkc_r = (i * tq + (r + 1) * ts + kc - 1) // kc
        thr_r = thr_ref[rows, 0:1]
        qa_stack = head_stack(qa_ref, rows, lambda h: (h % GQA_GROUP, h // GQA_GROUP))

        def score_pass(c, mrun):
            sel = sc_ref[c, rows, :] >= thr_r
            s = jnp.dot(qa_stack, kabt_ref[c], preferred_element_type=F32)
            s = jnp.where(sel[None], s.reshape(N_HEADS, ts, kc), -jnp.inf).reshape(N_HEADS * ts, kc)
            out = []
            for g in range(N_KV_HEADS):
                s_g = s[g * GQA_GROUP * ts:(g + 1) * GQA_GROUP * ts]
                s_ref[c, g] = s_g
                out.append(jnp.maximum(mrun[g], _lane_fold(s_g, jnp.maximum)))
            return tuple(out)
        npair = nkc_r // 2
        mrun = lax.fori_loop(0, npair, lambda c, mr: score_pass(2 * c + 1, score_pass(2 * c, mr)),
                             tuple(jnp.full((GQA_GROUP * ts, LANE), -jnp.inf, F32) for _ in range(N_KV_HEADS)))
        mrun = lax.fori_loop(2 * npair, nkc_r, score_pass, mrun)
        m = [jnp.max(mrun[g], axis=1, keepdims=True) for g in range(N_KV_HEADS)]

        def value_pass(c, acc):
            vchunk = vab_ref[pl.ds(pl.multiple_of(c * kc, kc), kc), :]
            out = []
            for g in range(N_KV_HEADS):
                vaug = jnp.where(v_lanes == g, vchunk, jnp.ones_like(vchunk))
                p = jnp.exp2(s_ref[c, g] - m[g]).astype(BF)
                out.append(acc[g] + jnp.dot(p, vaug, preferred_element_type=F32))
            return tuple(out)
        acc = lax.fori_loop(0, npair, lambda c, a: value_pass(2 * c + 1, value_pass(2 * c, a)),
                            tuple(jnp.zeros((GQA_GROUP * ts, LANE), F32) for _ in range(N_KV_HEADS)))
        acc = lax.fori_loop(2 * npair, nkc_r, value_pass, acc)

        for g in range(N_KV_HEADS):
            ones_lane = (1 - g) * HEAD_DIM
            o = acc[g] / acc[g][:, ones_lane:ones_lane + 1]
            for j in range(GQA_GROUP):
                h = g * GQA_GROUP + j
                out_ref[rows, h * HEAD_DIM:(h + 1) * HEAD_DIM] = (
                    o[j * ts:(j + 1) * ts, g * HEAD_DIM:(g + 1) * HEAD_DIM].astype(out_ref.dtype))


def _dsa_prompt(qi, misc, miscb_t, qa, kab_t, vab, batch, seq, tq, ts, kc):
    topk = min(TOPK_MAX, seq // 4)
    nq = seq // tq
    kern = functools.partial(_dsa_prompt_kernel, tq=tq, ts=ts, kc=kc, topk=topk, seq=seq)
    qrow = lambda b, i: (b * nq + i, 0)
    brow = lambda b, i: (b, 0)
    bchunks = pl.BlockSpec((seq // kc, LANE, kc), lambda b, i: (b, 0, 0))
    return pl.pallas_call(
        kern,
        grid=(batch, nq),
        in_specs=[pl.BlockSpec((tq, IDX_HEADS * IDX_DIM), qrow), pl.BlockSpec((tq, LANE), qrow),
                  bchunks, pl.BlockSpec((tq, N_HEADS * HEAD_DIM), qrow),
                  bchunks, pl.BlockSpec((seq, LANE), brow)],
        out_specs=pl.BlockSpec((tq, N_HEADS * HEAD_DIM), qrow),
        out_shape=jax.ShapeDtypeStruct((batch * seq, N_HEADS * HEAD_DIM), BF),
        scratch_shapes=[pltpu.VMEM((seq // kc, tq, kc), F32),
                        pltpu.VMEM((seq // kc, kc, tq), BF),
                        pltpu.VMEM((seq // kc, kc, tq), F32),
                        pltpu.VMEM((seq // kc, N_KV_HEADS, GQA_GROUP * ts, kc), F32),
                        pltpu.VMEM((tq, LANE), F32)],
        compiler_params=pltpu.CompilerParams(dimension_semantics=("parallel", "arbitrary"),
                                             vmem_limit_bytes=VMEM_LIMIT),
        name="dsa_prompt",
    )(qi, misc, miscb_t, qa, kab_t, vab)


def _log_sigmoid(x):
    return jnp.minimum(x, 0.0) - jnp.log1p(jnp.exp(-jnp.abs(x)))


def _pad_rows(x, n):
    if x.shape[0] == n:
        return x
    return jnp.concatenate([x, jnp.zeros((n - x.shape[0], x.shape[1]), x.dtype)], axis=0)


def _gla_kernel(*refs, chunk, nchunks, heads, has_s0):
    if has_s0:
        (misc_ref, wal_ref, bal_ref, q_ref, k_ref, v_ref, g_ref, gn_ref, s0_ref, o_ref, sout_ref, s_ref) = refs
    else:
        (misc_ref, wal_ref, bal_ref, q_ref, k_ref, v_ref, g_ref, gn_ref, o_ref, sout_ref, s_ref) = refs
        s0_ref = None
    t = pl.program_id(2)
    rows = chunk * nchunks
    cpad = max(chunk, 2 * SUBLANE)
    rpad = max(rows, 2 * SUBLANE)

    @pl.when(t == 0)
    def _():
        s_ref[...] = jnp.zeros(s_ref.shape, F32) if s0_ref is None else s0_ref[0]

    misc_b = _pad_rows(misc_ref[0], rpad).astype(BF)
    pos = lax.broadcasted_iota(I32, (rows, GLA_DK), 0) % chunk
    cums = []
    for hh in range(heads):
        la = jnp.dot(misc_b, wal_ref[hh], preferred_element_type=F32)[:rows]
        cum = _log_sigmoid(la + bal_ref[hh]) / GATE_TAU
        d = 1
        while d < chunk:
            cum = cum + jnp.where(pos >= d, pltpu.roll(cum, d, axis=0), 0.0)
            d *= 2
        cums.append(cum)

    causal = (lax.broadcasted_iota(I32, (cpad, cpad), 0) >= lax.broadcasted_iota(I32, (cpad, cpad), 1))
    gn = gn_ref[...]
    for c in range(nchunks):
        r = slice(c * chunk, (c + 1) * chunk)
        for hh in range(heads):
            kcols = slice(hh * GLA_DK, (hh + 1) * GLA_DK)
            vcols = slice(hh * GLA_DV, (hh + 1) * GLA_DV)
            cum_c = cums[hh][r]
            last = cum_c[chunk - 1:chunk]
            q_c = q_ref[0, r, kcols] * (GLA_DK ** -0.5)
            k_c = k_ref[0, r, kcols]
            v_b = _pad_rows(v_ref[0, r, vcols], cpad).astype(BF)
            qt = _pad_rows(q_c * jnp.exp(cum_c), cpad).astype(BF)
            kt = _pad_rows(k_c * jnp.exp(-cum_c), cpad).astype(BF)
            kd = _pad_rows(k_c * jnp.exp(last - cum_c), cpad).astype(BF)
            att = lax.dot_general(qt, kt, (((1,), (1,)), ((), ())), preferred_element_type=F32)
            att = jnp.where(causal, att, 0.0).astype(BF)
            s_old = s_ref[hh]
            o = (jnp.dot(att, v_b, preferred_element_type=F32)
                 + jnp.dot(qt, s_old.astype(BF), preferred_element_type=F32))[:chunk]
            decay = jnp.transpose(jnp.broadcast_to(jnp.exp(last), (GLA_DK, GLA_DK)))
            upd = lax.dot_general(kd, v_b, (((0,), (0,)), ((), ())), preferred_element_type=F32)
            s_ref[hh] = jnp.concatenate([decay] * (GLA_DV // GLA_DK), axis=1) * s_old + upd
            o = o * lax.rsqrt(jnp.mean(o * o, axis=-1, keepdims=True) + NORM_EPS) * gn
            gate = g_ref[0, r, vcols]
            o_ref[0, r, vcols] = (o * (gate * jax.nn.sigmoid(gate))).astype(o_ref.dtype)

    @pl.when(t == pl.num_programs(2) - 1)
    def _():
        sout_ref[0] = s_ref[...]


def _gla(misc, wal, bal, qb, kb, vb, gb, gn, s0, chunk, nchunks, heads, out_dtype):
    batch, seq, _ = misc.shape
    rows = chunk * nchunks
    head = lambda b, h, t: (b, t, h)
    in_specs = [pl.BlockSpec((1, rows, LANE), lambda b, h, t: (b, t, 0)),
                pl.BlockSpec((heads, LANE, GLA_DK), lambda b, h, t: (h, 0, 0)),
                pl.BlockSpec((heads, 1, GLA_DK), lambda b, h, t: (h, 0, 0)),
                pl.BlockSpec((1, rows, heads * GLA_DK), head), pl.BlockSpec((1, rows, heads * GLA_DK), head),
                pl.BlockSpec((1, rows, heads * GLA_DV), head), pl.BlockSpec((1, rows, heads * GLA_DV), head),
                pl.BlockSpec((1, GLA_DV), lambda b, h, t: (0, 0))]
    args = [misc, wal, bal, qb, kb, vb, gb, gn]
    state_spec = pl.BlockSpec((1, heads, GLA_DK, GLA_DV), lambda b, h, t: (b, h, 0, 0))
    if s0 is not None:
        in_specs.append(state_spec)
        args.append(s0)
    kern = functools.partial(_gla_kernel, chunk=chunk, nchunks=nchunks, heads=heads, has_s0=s0 is not None)
    return pl.pallas_call(
        kern,
        grid=(batch, GLA_HEADS // heads, seq // rows),
        in_specs=in_specs,
        out_specs=[pl.BlockSpec((1, rows, heads * GLA_DV), head), state_spec],
        out_shape=[jax.ShapeDtypeStruct((batch, seq, GLA_DV_TOT), out_dtype),
                   jax.ShapeDtypeStruct((batch, GLA_HEADS, GLA_DK, GLA_DV), F32)],
        scratch_shapes=[pltpu.VMEM((heads, GLA_DK, GLA_DV), F32)],
        compiler_params=pltpu.CompilerParams(dimension_semantics=("parallel", "parallel", "arbitrary"),
                                             vmem_limit_bytes=VMEM_LIMIT),
        name="gla",
    )(*args)


def _layer_norm(x, g, b):
    xc = x - jnp.mean(x, axis=-1, keepdims=True)
    var = jnp.mean(xc * xc, axis=-1, keepdims=True)
    return xc * lax.rsqrt(var + NORM_EPS) * g + b


def _post_kernel(x_ref, attn_ref, ob_ref, ga_ref, gb_ref, wao_ref, wgo_ref, wout_ref, ln1g_ref, ln1b_ref,
                 wf1_ref, wf2_ref, ln2g_ref, ln2b_ref, out_ref, *, ff_chunk):
    branch_a = jnp.dot(attn_ref[...], wao_ref[...], preferred_element_type=F32)
    branch_b = jnp.dot(ob_ref[...], wgo_ref[...], preferred_element_type=F32)
    merged = jax.nn.sigmoid(ga_ref[...]) * branch_a + jax.nn.sigmoid(gb_ref[...]) * branch_b
    mix = jnp.dot(merged.astype(BF), wout_ref[...], preferred_element_type=F32)
    h = _layer_norm(DN_ALPHA * x_ref[...] + mix, ln1g_ref[...], ln1b_ref[...])
    hb = h.astype(BF)
    ff = jnp.zeros(h.shape, F32)
    for c in range(D_FF // ff_chunk):
        cols = slice(c * ff_chunk, (c + 1) * ff_chunk)
        a = jnp.maximum(jnp.dot(hb, wf1_ref[:, cols], preferred_element_type=F32), 0.0)
        ff = ff + jnp.dot((a * a).astype(BF), wf2_ref[cols, :], preferred_element_type=F32)
    out_ref[...] = _layer_norm(DN_ALPHA * h + ff, ln2g_ref[...], ln2b_ref[...])


def _post(x2d, attn, ob, ga, gb, wao, wgo, wout, ln1g, ln1b, wf1, wf2, ln2g, ln2b, tm):
    m = x2d.shape[0]
    row = lambda width: pl.BlockSpec((tm, width), lambda i: (i, 0))
    return pl.pallas_call(
        functools.partial(_post_kernel, ff_chunk=1024),
        grid=(m // tm,),
        in_specs=[row(D_MODEL), row(N_HEADS * HEAD_DIM), row(GLA_DV_TOT), row(D_MODEL), row(D_MODEL),
                  _const_spec(wao.shape), _const_spec(wgo.shape), _const_spec(wout.shape),
                  _const_spec(ln1g.shape), _const_spec(ln1b.shape), _const_spec(wf1.shape), _const_spec(wf2.shape),
                  _const_spec(ln2g.shape), _const_spec(ln2b.shape)],
        out_specs=row(D_MODEL),
        out_shape=jax.ShapeDtypeStruct((m, D_MODEL), F32),
        compiler_params=pltpu.CompilerParams(dimension_semantics=("parallel",), vmem_limit_bytes=VMEM_LIMIT),
        name="post",
    )(x2d, attn, ob, ga, gb, wao, wgo, wout, ln1g, ln1b, wf1, wf2, ln2g, ln2b)


def _dsa_sample_kernel(pt_ref, qi_ref, misc_ref, qa_ref, ka_ref, va_ref,
                       ckidx_ref, ck_ref, cv_ref, out_ref,
                       ibuf, kbuf, vbuf, isem, ksem, vsem, keys_ref, *, t, n_pages, ppc, topk):
    b = pl.program_id(0)
    nch = n_pages // ppc
    kc = ppc * PAGE_SIZE
    past = n_pages * PAGE_SIZE
    hq = IDX_HEADS * t

    def page_copy(src_ref, buf, sem, ch, slot, p, seq_idx=b):
        return pltpu.make_async_copy(src_ref.at[pt_ref[seq_idx, ch * ppc + p]],
                                     buf.at[slot, ..., pl.ds(p * PAGE_SIZE, PAGE_SIZE)], sem.at[slot])

    def start_chunk(src_ref, buf, sem, ch, slot, seq_idx=b):
        for p in range(ppc):
            page_copy(src_ref, buf, sem, ch, slot, p, seq_idx).start()

    def wait_chunk(src_ref, buf, sem, ch, slot):
        for p in range(ppc):
            page_copy(src_ref, buf, sem, ch, slot, p).wait()

    def prefetch(streams, nslot):
        for ch in range(min(nslot - 1, nch)):
            for stream in streams:
                start_chunk(*stream, ch, ch % nslot)

    def arrive(streams, nslot, ch):
        if ch + nslot - 1 < nch:
            for stream in streams:
                start_chunk(*stream, ch + nslot - 1, (ch + nslot - 1) % nslot)
        for stream in streams:
            wait_chunk(*stream, ch, ch % nslot)
        return ch % nslot

    kv_streams = [(ck_ref, kbuf, ksem), (cv_ref, vbuf, vsem)]

    qi_all = jnp.concatenate([qi_ref[0, :, h * IDX_DIM:(h + 1) * IDX_DIM] for h in range(IDX_HEADS)],
                             axis=0).astype(BF)
    misc = misc_ref[0]
    w = misc[:, MISC_WI:MISC_WI + IDX_HEADS] * (IDX_HEADS ** -0.5)
    w_all = jnp.concatenate([w[:, h:h + 1] for h in range(IDX_HEADS)], axis=0)
    qpos = past + lax.broadcasted_iota(I32, (t, LANE), 0)

    nt_dims = (((1,), (1,)), ((), ()))

    def index_scores(d):
        r = jnp.maximum(d, 0.0) * w_all
        return jnp.sum(r.reshape(IDX_HEADS, t, r.shape[1]), axis=0)

    assert ibuf.shape[0] == nch

    def request_idx(seq_idx):
        for ch in range(nch):
            start_chunk(ckidx_ref, ibuf, isem, ch, ch, seq_idx)

    @pl.when(b == 0)
    def _():
        request_idx(b)

    for ch in range(nch):
        wait_chunk(ckidx_ref, ibuf, isem, ch, ch)
        d = jnp.dot(qi_all, ibuf[ch].astype(BF), preferred_element_type=F32)
        keys_ref[:, ch * kc:(ch + 1) * kc] = index_scores(d)

    @pl.when(b + 1 < pl.num_programs(0))
    def _():
        request_idx(b + 1)
    knew = _pad_rows(misc[:, :IDX_DIM], LANE).astype(BF)
    d_new = lax.dot_general(qi_all, knew, nt_dims, preferred_element_type=F32)
    new_pos = past + lax.broadcasted_iota(I32, (t, LANE), 1)
    keys_ref[:, past:past + LANE] = jnp.where(new_pos <= qpos, index_scores(d_new), -jnp.inf)

    prefetch(kv_streams, kbuf.shape[0])

    width = past + LANE
    col = lax.broadcasted_iota(I32, (t, width), 1)

    def count_where(indicator):
        return jnp.sum(_lane_fold(indicator(keys_ref[...])), axis=1, keepdims=True)

    thr_key = _radix_max(
        lambda cand: count_where(lambda s: jnp.where(s >= _key_to_f32(cand), 1.0, 0.0)),
        jnp.full((t, 1), INT_MIN, I32), 32, topk)
    thr = _key_to_f32(jnp.maximum(thr_key, KEY_LOWEST_FINITE))
    n_ge = count_where(lambda s: jnp.where(s >= thr, 1.0, 0.0))
    surplus = n_ge > topk

    @pl.when(jnp.max(jnp.where(surplus, 1, 0)) > 0)
    def _():
        need = topk - count_where(lambda s: jnp.where(s > thr, 1.0, 0.0))
        nbits = max(1, (width - 1).bit_length())
        cut = _tie_cut(lambda cand: count_where(
            lambda s: jnp.where(s == thr, jnp.where(col < cand, 1.0, 0.0), 0.0)), need, t, nbits)
        s = keys_ref[...]
        dropped = jnp.where(s == thr, jnp.where(col > cut, jnp.where(surplus, 1, 0), 0), 0)
        keys_ref[...] = jnp.where(dropped > 0, -jnp.inf, s)

    gt = GQA_GROUP * t
    qg = [jnp.concatenate([qa_ref[0, :, j * LANE + g * HEAD_DIM:j * LANE + (g + 1) * HEAD_DIM]
                           for j in range(GQA_GROUP)], axis=0).astype(BF)
          for g in range(N_KV_HEADS)]

    def attend(carry, scores, values, sel):
        m_old, l_old, acc_old = carry
        s = jnp.concatenate([scores(g) for g in range(N_KV_HEADS)], axis=0)
        n = s.shape[1]
        s = jnp.where(sel[None], s.reshape(N_HEADS, t, n), -jnp.inf).reshape(N_HEADS * t, n)
        m_new = jnp.maximum(m_old, jnp.max(s, axis=1, keepdims=True))
        alpha = jnp.exp2(m_old - m_new)
        p = jnp.exp2(s - m_new)
        l_new = alpha * l_old + jnp.sum(p, axis=1, keepdims=True)
        pb = p.astype(BF)
        pv = jnp.concatenate([values(g, pb[g * gt:(g + 1) * gt]) for g in range(N_KV_HEADS)], axis=0)
        return m_new, l_new, alpha * acc_old + pv

    carry = (jnp.full((hq, 1), NEG_BIG, F32), jnp.zeros((hq, 1), F32), jnp.zeros((hq, HEAD_DIM), F32))
    for ch in range(nch):
        slot = arrive(kv_streams, kbuf.shape[0], ch)
        sel = keys_ref[:, ch * kc:(ch + 1) * kc] >= thr
        carry = attend(
            carry,
            lambda g: jnp.dot(qg[g], kbuf[slot, g].astype(BF), preferred_element_type=F32),
            lambda g, pb: lax.dot_general(pb, vbuf[slot, g].astype(BF), nt_dims, preferred_element_type=F32),
            sel)
    sel_new = keys_ref[:, past:past + LANE] >= thr
    k_new = _pad_rows(ka_ref[0], LANE)
    v_new = _pad_rows(va_ref[0], LANE)
    carry = attend(
        carry,
        lambda g: lax.dot_general(qg[g], k_new[:, g * HEAD_DIM:(g + 1) * HEAD_DIM].astype(BF), nt_dims,
                                  preferred_element_type=F32),
        lambda g, pb: jnp.dot(pb, v_new[:, g * HEAD_DIM:(g + 1) * HEAD_DIM].astype(BF), preferred_element_type=F32),
        sel_new)
    _, l_fin, acc_fin = carry
    o = acc_fin / l_fin
    for h in range(N_HEADS):
        out_ref[0, :, h * HEAD_DIM:(h + 1) * HEAD_DIM] = o[h * t:(h + 1) * t].astype(out_ref.dtype)


def _dsa_sample(page_table, qi, misc, qa, ka, va, cache_kidx, cache_k, cache_v, ppc):
    batch, t, _ = misc.shape
    n_pages = page_table.shape[1]
    past = n_pages * PAGE_SIZE
    topk = min(TOPK_MAX, (past + t) // 4)
    kern = functools.partial(_dsa_sample_kernel, t=t, n_pages=n_pages, ppc=ppc, topk=topk)
    idx_slots = n_pages // ppc
    kv_slots = min(4, n_pages // ppc)
    row = lambda width: pl.BlockSpec((1, t, width), lambda b, pt: (b, 0, 0))
    hbm = pl.BlockSpec(memory_space=pl.ANY)
    grid_spec = pltpu.PrefetchScalarGridSpec(
        num_scalar_prefetch=1,
        grid=(batch,),
        in_specs=[row(IDX_HEADS * IDX_DIM), row(LANE), row(N_HEADS * HEAD_DIM), row(LANE), row(LANE),
                  hbm, hbm, hbm],
        out_specs=row(N_HEADS * HEAD_DIM),
        scratch_shapes=[pltpu.VMEM((idx_slots, IDX_DIM, ppc * PAGE_SIZE), F32),
                        pltpu.VMEM((kv_slots, N_KV_HEADS, HEAD_DIM, ppc * PAGE_SIZE), F32),
                        pltpu.VMEM((kv_slots, N_KV_HEADS, HEAD_DIM, ppc * PAGE_SIZE), F32),
                        pltpu.SemaphoreType.DMA((idx_slots,)), pltpu.SemaphoreType.DMA((kv_slots,)),
                        pltpu.SemaphoreType.DMA((kv_slots,)),
                        pltpu.VMEM((t, past + LANE), F32)],
    )
    return pl.pallas_call(
        kern,
        grid_spec=grid_spec,
        out_shape=jax.ShapeDtypeStruct((batch, t, N_HEADS * HEAD_DIM), F32),
        compiler_params=pltpu.CompilerParams(dimension_semantics=("arbitrary",), vmem_limit_bytes=VMEM_LIMIT),
        name="dsa_sample",
    )(page_table, qi, misc, qa, ka, va, cache_kidx, cache_k, cache_v)


def _row_tile(m, cap):
    tm = min(m, cap)
    while m % tm:
        tm //= 2
    return tm


def _gla_and_post(x2d, attn, misc, gla_in, gates, weights, batch, seq, gla_s0, gla_chunk, gla_nchunks, gla_heads,
                  gla_out_dtype, tm):
    (_, wal, bal, gn, wao, wgo, wout, ln1g, ln1b, wf1, wf2, ln2g, ln2b) = weights
    b3 = lambda a: a.reshape(batch, seq, a.shape[-1])
    qb, kb, vb, gb = gla_in
    ob, s_new = _gla(b3(misc), wal, bal, b3(qb), b3(kb), b3(vb), b3(gb), gn, gla_s0,
                     gla_chunk, gla_nchunks, gla_heads, gla_out_dtype)
    ob = ob.reshape(batch * seq, GLA_DV_TOT).astype(BF)
    y = _post(x2d, attn, ob, *gates, wao, wgo, wout, ln1g, ln1b, wf1, wf2, ln2g, ln2b, tm)
    return y, s_new


def kernel(x_prompt, x_sample, cache_k, cache_v, cache_kidx, state_gla, page_table,
           w_in, w_alpha2, b_alpha, gla_norm_g, w_attn_o, w_gla_o, w_out,
           ln1_g, ln1_b, w_ff1, w_ff2, ln2_g, ln2_b):
    assert w_in.shape[0] == DEPTH == 1
    bp, tp, _ = x_prompt.shape
    bs, ts, _ = x_sample.shape
    n_pool = cache_k.shape[1]
    l = 0
    wal = jnp.zeros((LANE, GLA_DK_TOT), F32).at[MISC_AB:MISC_AB + GATE_RANK].set(w_alpha2[l])
    wal = wal.reshape(LANE, GLA_HEADS, GLA_DK).transpose(1, 0, 2).astype(BF)
    weights = (_pack_w_in(w_in[l]), wal, b_alpha[l].reshape(GLA_HEADS, 1, GLA_DK), gla_norm_g[l].reshape(1, GLA_DV),
               w_attn_o[l].astype(BF), w_gla_o[l].astype(BF), w_out[l].astype(BF),
               ln1_g[l].reshape(1, D_MODEL), ln1_b[l].reshape(1, D_MODEL),
               w_ff1[l].astype(BF), w_ff2[l].astype(BF), ln2_g[l].reshape(1, D_MODEL), ln2_b[l].reshape(1, D_MODEL))

    xp = x_prompt.reshape(bp * tp, D_MODEL)
    kc = min(512, tp)
    tq = min(256, tp)
    (qa, ka_t, va_t, kab_t, vab, qi, misc, ki_t, miscb_t, *rest) = _proj(xp, weights[0], kc, token_minor_seq=tp)
    attn = _dsa_prompt(qi, misc, miscb_t, qa, kab_t, vab, batch=bp, seq=tp, tq=tq, ts=min(128, tq), kc=kc)
    gchunk = min(GLA_CHUNK, tp)
    y_p, s_p = _gla_and_post(xp, attn, misc, rest[:4], rest[4:], weights, bp, tp, None, gchunk,
                             min(8, tp // gchunk), GLA_HEADS, BF, kc)

    def kv_t(a):
        return jnp.transpose(a.reshape(bp, N_KV_HEADS, HEAD_DIM, tp), (0, 3, 1, 2))[None]

    xs = x_sample.reshape(bs * ts, D_MODEL)
    tm_s = _row_tile(bs * ts, 512)
    (qa, ka, va, qi, misc, *rest) = _proj(xs, weights[0], tm_s)
    ckidx = jnp.transpose(cache_kidx[l], (0, 2, 1))
    ck = jnp.transpose(cache_k[l], (0, 2, 3, 1))
    cv = jnp.transpose(cache_v[l], (0, 2, 3, 1))
    f3 = lambda a: a.astype(F32).reshape(bs, ts, a.shape[-1])
    attn = _dsa_sample(page_table, f3(qi), f3(misc), f3(qa), f3(ka), f3(va), ckidx, ck, cv,
                       ppc=min(16, page_table.shape[1]))
    attn = attn.reshape(bs * ts, N_HEADS * HEAD_DIM).astype(BF)
    y_s, s_s = _gla_and_post(xs, attn, misc, rest[:4], rest[4:], weights, bs, ts, state_gla[l], ts, 1, GLA_HEADS,
                             F32, tm_s)

    def kv(a):
        return a.reshape(1, bs, ts, N_KV_HEADS, HEAD_DIM)

    return (y_p.reshape(bp, tp, D_MODEL), y_s.reshape(bs, ts, D_MODEL),
            kv_t(ka_t), kv_t(va_t), jnp.transpose(ki_t, (0, 2, 1))[None], s_p[None],
            kv(ka), kv(va), misc[:, :IDX_DIM].reshape(1, bs, ts, IDX_DIM), s_s[None])
```
